```python
import math
import jax, jax.numpy as jnp
from jax import lax
import numpy as np

D_MODEL = 1024
BATCH = 8
SEQ = 2048
DEPTH = 4
DEC_BATCH = 32
DEC_SEQ = 1
PAST_LEN = 8192
PAGE_SIZE = 128

N_ATTN_LAYERS = (DEPTH + 1) // 2
N_SSM_LAYERS = DEPTH // 2
SB_HEADS = 8
SB_HEAD_DIM = 64
SB_WIDTH = SB_HEADS * SB_HEAD_DIM
Q_BLOCK = 128
SB_BIAS_INIT = -6.0
CONV_WIDTH = 512
CONV_K = 31
IN_MIX = 3 * SB_WIDTH + 2 * CONV_WIDTH
OUT_MIX = SB_WIDTH + CONV_WIDTH
SSM_GROUP = 16
SSM_GROUPS = D_MODEL // SSM_GROUP
SSM_STATE = 64
N_EXPERTS = 16
N_EXPERT_GROUPS = 4
EXPERTS_PER_GROUP = N_EXPERTS // N_EXPERT_GROUPS
TOP_K = 2
EXPERT_DIM = 512
N_ADA = 6
EPS = 1e-6
LN_EPS = 1e-5

kernel_name = 'cond_hybrid_stickbreak_conv_s5_moe_step'


def rmsnorm(x, w):
    xf = x.astype(jnp.float32)
    y = xf * lax.rsqrt(jnp.mean(xf * xf, axis=-1, keepdims=True) + EPS)
    return (y * w.astype(jnp.float32)).astype(x.dtype)


def ada_params(c, w_ada, b_ada):
    a = jax.nn.silu(c) @ w_ada + b_ada
    return jnp.split(a, N_ADA, axis=-1)


def modulate(h, shift, scale):
    return h * (1 + scale[:, None, :]) + shift[:, None, :]


def causal_dwconv(u, prefix, w, b):
    full = jnp.concatenate([prefix.astype(u.dtype), u], axis=1)
    y = lax.conv_general_dilated(full, w.astype(u.dtype)[:, None, :], window_strides=(1,),
                                 padding='VALID', dimension_numbers=('NWC', 'WIO', 'NWC'),
                                 feature_group_count=u.shape[-1])
    return y + b, full[:, full.shape[1] - (CONV_K - 1):]


def conformer_conv(a, g, prefix, conv_w, conv_b, ln_w, ln_b):
    u = a * jax.nn.sigmoid(g)
    y, new_prefix = causal_dwconv(u, prefix, conv_w, conv_b)
    yf = y.astype(jnp.float32)
    mu = jnp.mean(yf, axis=-1, keepdims=True)
    var = jnp.mean(jnp.square(yf - mu), axis=-1, keepdims=True)
    yn = (yf - mu) * lax.rsqrt(var + LN_EPS) * ln_w.astype(jnp.float32) + ln_b.astype(jnp.float32)
    return jax.nn.silu(yn).astype(a.dtype), new_prefix


def sb_block(q, k, v, q_pos, k_pos, bias):
    z = jnp.einsum('bqhd,bshd->bhqs', q.astype(jnp.float32), k.astype(jnp.float32)) / math.sqrt(SB_HEAD_DIM)
    z = z + bias.astype(jnp.float32)[None, :, None, None]
    mask = k_pos[None, :] < q_pos[:, None]
    log_1m = jnp.where(mask, jax.nn.log_sigmoid(-z), 0.0)
    after = lax.cumsum(log_1m, axis=3, reverse=True) - log_1m
    w = jnp.where(mask, jnp.exp(jax.nn.log_sigmoid(z) + after), 0.0)
    return jnp.einsum('bhqs,bshd->bqhd', w, v.astype(jnp.float32)).astype(v.dtype)


def sb_attn_prompt(q, k, v, bias):
    b, l, h, dh = q.shape
    nb = l // Q_BLOCK
    qb = q.reshape(b, nb, Q_BLOCK, h, dh).transpose(1, 0, 2, 3, 4)
    k_pos = jnp.arange(l)

    def one_block(args):
        blk, q_blk = args
        q_pos = blk * Q_BLOCK + jnp.arange(Q_BLOCK)
        return sb_block(q_blk, k, v, q_pos, k_pos, bias)

    out = lax.map(one_block, (jnp.arange(nb), qb))
    return out.transpose(1, 0, 2, 3, 4).reshape(b, l, h, dh)


def sb_attn_sample(q, k_new, v_new, k_past, v_past, bias):
    p = k_past.shape[1]
    t = q.shape[1]
    k = jnp.concatenate([k_past.astype(k_new.dtype), k_new], axis=1)
    v = jnp.concatenate([v_past.astype(v_new.dtype), v_new], axis=1)
    return sb_block(q, k, v, p + jnp.arange(t), jnp.arange(p + t), bias)


def even_mixer(h, w_in, w_out, sb_bias, conv_w, conv_b, ln_w, ln_b, prefix, attn_fn):
    b, l, _ = h.shape
    proj = h @ w_in
    q, k, v, a, g = jnp.split(proj, [SB_WIDTH, 2 * SB_WIDTH, 3 * SB_WIDTH, 3 * SB_WIDTH + CONV_WIDTH], axis=-1)
    q = q.reshape(b, l, SB_HEADS, SB_HEAD_DIM)
    k = k.reshape(b, l, SB_HEADS, SB_HEAD_DIM)
    v = v.reshape(b, l, SB_HEADS, SB_HEAD_DIM)
    o_attn = attn_fn(q, k, v, sb_bias).reshape(b, l, SB_WIDTH)
    o_conv, new_prefix = conformer_conv(a, g, prefix, conv_w, conv_b, ln_w, ln_b)
    out = jnp.concatenate([o_attn, o_conv], axis=-1) @ w_out
    return out, k, v, new_prefix


def s5_mixer(u, h0_re, h0_im, lam_re, lam_im, b_re, b_im, c_re, c_im, d, log_dt, w_glu):
    f32 = jnp.float32
    b, l, dm = u.shape
    uf = u.astype(f32).reshape(b, l, SSM_GROUPS, SSM_GROUP)
    lam = lax.complex(lam_re.astype(f32), lam_im.astype(f32))
    dt = jnp.exp(log_dt.astype(f32))[:, None]
    a_bar = jnp.exp(lam * dt)
    bmat = lax.complex(b_re.astype(f32), b_im.astype(f32))
    b_bar = ((a_bar - 1) / lam)[..., None] * bmat
    bu = jnp.einsum('blgc,gpc->blgp', uf, b_bar)
    a_seq = jnp.broadcast_to(a_bar, bu.shape)

    def combine(e1, e2):
        a1, x1 = e1
        a2, x2 = e2
        return a1 * a2, a2 * x1 + x2

    a_cum, x_cum = lax.associative_scan(combine, (a_seq, bu), axis=1)
    h0 = lax.complex(h0_re.astype(f32), h0_im.astype(f32))
    h_seq = a_cum * h0[:, None] + x_cum
    cmat = lax.complex(c_re.astype(f32), c_im.astype(f32))
    y = jnp.einsum('gcp,blgp->blgc', cmat, h_seq).real + d.astype(f32).reshape(SSM_GROUPS, SSM_GROUP) * uf
    y = jax.nn.gelu(y.reshape(b, l, dm)).astype(u.dtype)
    za, zb = jnp.split(y @ w_glu, 2, axis=-1)
    h_last = h_seq[:, -1]
    return za * jax.nn.sigmoid(zb), h_last.real, h_last.imag


def moe(h, w_router, b_router, w_gate, w_up, w_down):
    f32 = jnp.float32
    b, l, dm = h.shape
    t = h.reshape(-1, dm)
    scores = jax.nn.sigmoid((t @ w_router).astype(f32))
    sel = (scores + b_router.astype(f32)).reshape(-1, N_EXPERT_GROUPS, EXPERTS_PER_GROUP)
    group_score = lax.top_k(sel, TOP_K)[0].sum(-1)
    g_best = jnp.argmax(group_score, axis=-1)
    sel_in = jnp.take_along_axis(sel, g_best[:, None, None], axis=1)[:, 0]
    _, local = lax.top_k(sel_in, TOP_K)
    idx = g_best[:, None] * EXPERTS_PER_GROUP + local
    wk = jnp.take_along_axis(scores, idx, axis=1)
    wk = wk / jnp.sum(wk, axis=-1, keepdims=True)
    gates = jnp.sum(jax.nn.one_hot(idx, N_EXPERTS, dtype=f32) * wk[..., None], axis=1)
    hg = jnp.einsum('td,edf->tef', t, w_gate)
    hu = jnp.einsum('td,edf->tef', t, w_up)
    act = jax.nn.silu(hg) * hu * gates[..., None].astype(t.dtype)
    return jnp.einsum('tef,efd->td', act, w_down).reshape(b, l, dm)


def setup_inputs(seed: int = 0) -> dict:
    key = jax.random.key(seed)
    ks = jax.random.split(key, 40)
    f32 = jnp.float32
    n_pages = PAST_LEN // PAGE_SIZE
    n_used = DEC_BATCH * n_pages
    n_phys = (n_used * 5 + 3) // 4

    def nrm(k, shape, s):
        return s * jax.random.normal(k, shape, f32)

    page_table = jax.random.permutation(ks[6], n_phys)[:n_used].reshape(DEC_BATCH, n_pages).astype(jnp.int32)
    n_idx = jnp.arange(SSM_STATE, dtype=f32)
    return {
        'x_prompt': nrm(ks[0], (BATCH, SEQ, D_MODEL), 1.0),
        'x_sample': nrm(ks[1], (DEC_BATCH, DEC_SEQ, D_MODEL), 1.0),
        'c_prompt': nrm(ks[2], (BATCH, D_MODEL), 1.0),
        'c_sample': nrm(ks[3], (DEC_BATCH, D_MODEL), 1.0),
        'cache_k': nrm(ks[4], (N_ATTN_LAYERS, n_phys, PAGE_SIZE, SB_HEADS, SB_HEAD_DIM), 1.0),
        'cache_v': nrm(ks[5], (N_ATTN_LAYERS, n_phys, PAGE_SIZE, SB_HEADS, SB_HEAD_DIM), 1.0),
        'page_table': page_table,
        'state_conv': nrm(ks[7], (N_ATTN_LAYERS, DEC_BATCH, CONV_K - 1, CONV_WIDTH), 0.5),
        'state_ssm_re': nrm(ks[8], (N_SSM_LAYERS, DEC_BATCH, SSM_GROUPS, SSM_STATE), 0.5),
        'state_ssm_im': nrm(ks[9], (N_SSM_LAYERS, DEC_BATCH, SSM_GROUPS, SSM_STATE), 0.5),
        'w_ada': nrm(ks[10], (DEPTH, D_MODEL, N_ADA * D_MODEL), 0.5 * D_MODEL ** -0.5),
        'b_ada': nrm(ks[11], (DEPTH, N_ADA * D_MODEL), 0.02),
        'norm_w': 1.0 + nrm(ks[12], (DEPTH, 2, D_MODEL), 0.01),
        'final_norm_w': 1.0 + nrm(ks[13], (D_MODEL,), 0.01),
        'w_in_mix': nrm(ks[14], (N_ATTN_LAYERS, D_MODEL, IN_MIX), D_MODEL ** -0.5),
        'w_out_mix': nrm(ks[15], (N_ATTN_LAYERS, OUT_MIX, D_MODEL), OUT_MIX ** -0.5),
        'sb_bias': SB_BIAS_INIT + nrm(ks[34], (N_ATTN_LAYERS, SB_HEADS), 0.1),
        'conv_w': nrm(ks[16], (N_ATTN_LAYERS, CONV_K, CONV_WIDTH), CONV_K ** -0.5),
        'conv_b': nrm(ks[17], (N_ATTN_LAYERS, CONV_WIDTH), 0.01),
        'conv_ln_w': 1.0 + nrm(ks[18], (N_ATTN_LAYERS, CONV_WIDTH), 0.01),
        'conv_ln_b': nrm(ks[19], (N_ATTN_LAYERS, CONV_WIDTH), 0.01),
        'ssm_lambda_re': -0.5 + nrm(ks[20], (N_SSM_LAYERS, SSM_GROUPS, SSM_STATE), 0.01),
        'ssm_lambda_im': math.pi * n_idx + nrm(ks[21], (N_SSM_LAYERS, SSM_GROUPS, SSM_STATE), 0.01),
        'ssm_b_re': nrm(ks[22], (N_SSM_LAYERS, SSM_GROUPS, SSM_STATE, SSM_GROUP), (2 * SSM_GROUP) ** -0.5),
        'ssm_b_im': nrm(ks[23], (N_SSM_LAYERS, SSM_GROUPS, SSM_STATE, SSM_GROUP), (2 * SSM_GROUP) ** -0.5),
        'ssm_c_re': nrm(ks[24], (N_SSM_LAYERS, SSM_GROUPS, SSM_GROUP, SSM_STATE), (2 * SSM_STATE) ** -0.5),
        'ssm_c_im': nrm(ks[25], (N_SSM_LAYERS, SSM_GROUPS, SSM_GROUP, SSM_STATE), (2 * SSM_STATE) ** -0.5),
        'ssm_d': nrm(ks[26], (N_SSM_LAYERS, D_MODEL), 1.0),
        'ssm_log_dt': jax.random.uniform(ks[27], (N_SSM_LAYERS, SSM_GROUPS), f32, math.log(0.001), math.log(0.1)),
        'w_glu': nrm(ks[28], (N_SSM_LAYERS, D_MODEL, 2 * D_MODEL), D_MODEL ** -0.5),
        'w_router': nrm(ks[29], (D_MODEL, N_EXPERTS), D_MODEL ** -0.5),
        'b_router': nrm(ks[30], (N_EXPERTS,), 0.01),
        'w_gate': nrm(ks[31], (DEPTH, N_EXPERTS, D_MODEL, EXPERT_DIM), D_MODEL ** -0.5),
        'w_up': nrm(ks[32], (DEPTH, N_EXPERTS, D_MODEL, EXPERT_DIM), D_MODEL ** -0.5),
        'w_down': nrm(ks[33], (DEPTH, N_EXPERTS, EXPERT_DIM, D_MODEL), EXPERT_DIM ** -0.5),
    }


def reference(x_prompt, x_sample, c_prompt, c_sample, cache_k, cache_v, page_table, state_conv,
              state_ssm_re, state_ssm_im, w_ada, b_ada, norm_w, final_norm_w, w_in_mix, w_out_mix,
              sb_bias, conv_w, conv_b, conv_ln_w, conv_ln_b, ssm_lambda_re, ssm_lambda_im, ssm_b_re,
              ssm_b_im, ssm_c_re, ssm_c_im, ssm_d, ssm_log_dt, w_glu, w_router, b_router, w_gate,
              w_up, w_down):
    n_dec = x_sample.shape[0]
    n_prompt = x_prompt.shape[0]
    past_len = page_table.shape[1] * cache_k.shape[2]
    xp, xs = x_prompt, x_sample
    kp_l, vp_l, ks_l, vs_l, cvp_l, cvs_l = [], [], [], [], [], []
    srp_l, sip_l, srs_l, sis_l = [], [], [], []
    for layer in range(DEPTH):
        ap = ada_params(c_prompt, w_ada[layer], b_ada[layer])
        asm = ada_params(c_sample, w_ada[layer], b_ada[layer])
        hp = modulate(rmsnorm(xp, norm_w[layer, 0]), ap[0], ap[1])
        hs = modulate(rmsnorm(xs, norm_w[layer, 0]), asm[0], asm[1])
        i = layer // 2
        if layer % 2 == 0:
            zero_prefix = jnp.zeros((n_prompt, CONV_K - 1, CONV_WIDTH), xp.dtype)
            mp, kp, vp, cvp = even_mixer(hp, w_in_mix[i], w_out_mix[i], sb_bias[i], conv_w[i], conv_b[i],
                                         conv_ln_w[i], conv_ln_b[i], zero_prefix, sb_attn_prompt)
            k_past = cache_k[i][page_table].reshape(n_dec, past_len, SB_HEADS, SB_HEAD_DIM)
            v_past = cache_v[i][page_table].reshape(n_dec, past_len, SB_HEADS, SB_HEAD_DIM)

            def attn_sample(q, k, v, bias, k_past=k_past, v_past=v_past):
                return sb_attn_sample(q, k, v, k_past, v_past, bias)

            ms, ks_, vs_, cvs = even_mixer(hs, w_in_mix[i], w_out_mix[i], sb_bias[i], conv_w[i], conv_b[i],
                                           conv_ln_w[i], conv_ln_b[i], state_conv[i], attn_sample)
            kp_l.append(kp)
            vp_l.append(vp)
            ks_l.append(ks_)
            vs_l.append(vs_)
            cvp_l.append(cvp)
            cvs_l.append(cvs)
        else:
            zero_state = jnp.zeros((n_prompt, SSM_GROUPS, SSM_STATE), jnp.float32)
            ssm_w = (ssm_lambda_re[i], ssm_lambda_im[i], ssm_b_re[i], ssm_b_im[i], ssm_c_re[i],
                     ssm_c_im[i], ssm_d[i], ssm_log_dt[i], w_glu[i])
            mp, srp, sip = s5_mixer(hp, zero_state, zero_state, *ssm_w)
            ms, srs, sis = s5_mixer(hs, state_ssm_re[i], state_ssm_im[i], *ssm_w)
            srp_l.append(srp)
            sip_l.append(sip)
            srs_l.append(srs)
            sis_l.append(sis)
        xp = xp + ap[2][:, None, :] * mp
        xs = xs + asm[2][:, None, :] * ms
        hp = modulate(rmsnorm(xp, norm_w[layer, 1]), ap[3], ap[4])
        hs = modulate(rmsnorm(xs, norm_w[layer, 1]), asm[3], asm[4])
        xp = xp + ap[5][:, None, :] * moe(hp, w_router, b_router, w_gate[layer], w_up[layer], w_down[layer])
        xs = xs + asm[5][:, None, :] * moe(hs, w_router, b_router, w_gate[layer], w_up[layer], w_down[layer])
    y_prompt = rmsnorm(xp, final_norm_w)
    y_sample = rmsnorm(xs, final_norm_w)
    return (y_prompt, y_sample,
            jnp.stack(kp_l), jnp.stack(vp_l), jnp.stack(ks_l), jnp.stack(vs_l),
            jnp.stack(cvp_l), jnp.stack(cvs_l),
            jnp.stack(srp_l), jnp.stack(sip_l), jnp.stack(srs_l), jnp.stack(sis_l))
```

```python
import functools
import math

import jax
import jax.numpy as jnp
from jax import lax
from jax.experimental import pallas as pl
from jax.experimental.pallas import tpu as pltpu

F32 = jnp.float32
BF16 = jnp.bfloat16
HIGHEST = lax.Precision.HIGHEST

EPS = 1e-6
LN_EPS = 1e-5
SB_HEADS = 8
SB_HEAD_DIM = 64
SB_WIDTH = SB_HEADS * SB_HEAD_DIM
CONV_WIDTH = 512
CONV_K = 31
SSM_GROUP = 16
SSM_STATE = 64
N_EXPERTS = 16
EXPERTS_PER_GROUP = 4
N_ADA = 6

LANES = 128
SUBLANES = 8
S5_CHUNK = 16
NT_DIMS = (((1,), (1,)), ((), ()))


def _dot(a, b):
    return jnp.dot(a, b, preferred_element_type=F32)


def _dotf(a, b):
    return jnp.dot(a, b, preferred_element_type=F32, precision=HIGHEST)


def _mm(a, w):
    if w.dtype == BF16:
        return _dot(a.astype(BF16), w)
    return _dotf(a, w)


def _silu(x):
    return x * jax.nn.sigmoid(x)


def _rms_mod(x, nw, shift, scale):
    y = x * lax.rsqrt(jnp.mean(x * x, axis=-1, keepdims=True) + EPS) * nw
    return y * (1 + scale) + shift


def _mod_spec(a, tm):
    d = a.shape[-1]
    if a.shape[1] == 1:
        return pl.BlockSpec((1, 1, d), lambda i, j, *_: (i, 0, 0))
    return pl.BlockSpec((1, tm, d), lambda i, j, *_: (i, j, 0))


def _split_bf16(x):
    hi = x.astype(BF16)
    lo = (x - hi.astype(F32)).astype(BF16)
    return hi, lo


def _ada_kernel(c_ref, w_ref, b_ref, o_ref):
    o_ref[0] = _dotf(_silu(c_ref[...]), w_ref[0]) + b_ref[0]


def ada_all(c, w_ada, b_ada):
    depth, d, n = w_ada.shape
    r = c.shape[0]
    tn = 1024
    return pl.pallas_call(
        _ada_kernel,
        grid=(depth, n // tn),
        in_specs=[pl.BlockSpec((r, d), lambda l, j: (0, 0)),
                  pl.BlockSpec((1, d, tn), lambda l, j: (l, 0, j)),
                  pl.BlockSpec((1, 1, tn), lambda l, j: (l, 0, j))],
        out_specs=pl.BlockSpec((1, r, tn), lambda l, j: (l, 0, j)),
        out_shape=jax.ShapeDtypeStruct((depth, r, n), F32),
        name="ada",
    )(c, w_ada, b_ada.reshape(depth, 1, n))


def _inproj_kernel(x_ref, nw_ref, sh_ref, sc_ref, w_ref, q_ref, k_ref, v_ref, u_ref):
    h = _rms_mod(x_ref[0], nw_ref[...], sh_ref[0], sc_ref[0])
    p = _mm(h, w_ref[...])
    w = SB_WIDTH
    q_ref[0] = (p[:, :w] * (1.0 / math.sqrt(SB_HEAD_DIM))).astype(q_ref.dtype)
    k_ref[0] = p[:, w:2 * w]
    v_ref[0] = p[:, 2 * w:3 * w]
    a = p[:, 3 * w:3 * w + CONV_WIDTH]
    g = p[:, 3 * w + CONV_WIDTH:]
    u_ref[0] = a * jax.nn.sigmoid(g)


def inproj(x, nw, shift, scale, w_in, tm, q_dtype):
    b, l, d = x.shape
    n = w_in.shape[1]
    row = lambda i, j: (i, j, 0)
    o = jax.ShapeDtypeStruct((b, l, SB_WIDTH), F32)
    return pl.pallas_call(
        _inproj_kernel,
        grid=(b, l // tm),
        in_specs=[pl.BlockSpec((1, tm, d), row),
                  pl.BlockSpec((1, d), lambda i, j: (0, 0)),
                  _mod_spec(shift, tm), _mod_spec(scale, tm),
                  pl.BlockSpec((d, n), lambda i, j: (0, 0))],
        out_specs=[pl.BlockSpec((1, tm, SB_WIDTH), row)] * 4,
        out_shape=[jax.ShapeDtypeStruct((b, l, SB_WIDTH), q_dtype), o, o, o],
        name="inproj",
    )(x, nw.reshape(1, d), shift, scale, w_in)


ATT_TQ = 256
ATT_TK = 128


def _softplus(z):
    return jnp.maximum(z, 0.0) + jnp.log1p(jnp.exp(-jnp.abs(z)))


def _sb_attn_kernel(bias_ref, q_ref, k_ref, v_ref, o_ref, acc_ref, car_ref):
    tq, tk = ATT_TQ, ATT_TK
    hp = pl.program_id(1)
    i = pl.program_id(2)
    q2 = q_ref[0]
    lane = lax.broadcasted_iota(jnp.int32, (tq, LANES), 1)
    head_of_lane = lane // SB_HEAD_DIM
    qm = [jnp.where(head_of_lane == hh, q2, jnp.zeros_like(q2)) for hh in range(2)]
    bias = [bias_ref[2 * hp + hh] for hh in range(2)]
    r_io = lax.broadcasted_iota(jnp.int32, (tk, tk), 0)
    c_io = lax.broadcasted_iota(jnp.int32, (tk, tk), 1)
    tri = (r_io > c_io).astype(BF16)
    acc_ref[...] = jnp.zeros_like(acc_ref)
    car_ref[...] = jnp.zeros_like(car_ref)
    qpos = i * tq + lax.broadcasted_iota(jnp.int32, (tq, tk), 0)
    kcol = lax.broadcasted_iota(jnp.int32, (tq, tk), 1)

    def block(j, masked):
        start = pl.multiple_of(j * tk, tk)
        kb = k_ref[0, pl.ds(start, tk), :].astype(BF16)
        vb = v_ref[0, pl.ds(start, tk), :].astype(BF16)
        mask = (j * tk + kcol) < qpos if masked else None
        for hh in range(2):
            z = lax.dot_general(qm[hh], kb, NT_DIMS, preferred_element_type=F32) + bias[hh]
            sp = _softplus(z)
            l1m = -sp
            if masked:
                l1m = jnp.where(mask, l1m, 0.0)
            hi, lo = _split_bf16(l1m)
            later = _dot(hi, tri) + _dot(lo, tri)
            w = jnp.exp(z - sp + car_ref[hh] + later)
            if masked:
                w = jnp.where(mask, w, 0.0)
            acc_ref[hh] += _dot(w.astype(BF16), vb)
            car_ref[hh] += jnp.sum(l1m, axis=-1, keepdims=True)

    nfull = i * (tq // tk)
    for jj in range(tq // tk - 1, -1, -1):
        block(nfull + jj, True)

    def body(t, c):
        block(nfull - 1 - t, False)
        return c

    lax.fori_loop(0, nfull, body, 0)
    o_ref[0] = jnp.where(head_of_lane == 0, acc_ref[0], acc_ref[1]).astype(o_ref.dtype)


def sb_attn_prompt(q, k, v, bias):
    b, l, w = q.shape
    tq = ATT_TQ
    grid_spec = pltpu.PrefetchScalarGridSpec(
        num_scalar_prefetch=0,
        grid=(b, w // LANES, l // tq),
        in_specs=[pl.BlockSpec(memory_space=pltpu.SMEM),
                  pl.BlockSpec((1, tq, LANES), lambda bi, hp, i: (bi, i, hp)),
                  pl.BlockSpec((1, l, LANES), lambda bi, hp, i: (bi, 0, hp)),
                  pl.BlockSpec((1, l, LANES), lambda bi, hp, i: (bi, 0, hp))],
        out_specs=pl.BlockSpec((1, tq, LANES), lambda bi, hp, i: (bi, i, hp)),
        scratch_shapes=[pltpu.VMEM((2, tq, LANES), F32), pltpu.VMEM((2, tq, ATT_TK), F32)],
    )
    return pl.pallas_call(
        _sb_attn_kernel,
        grid_spec=grid_spec,
        out_shape=jax.ShapeDtypeStruct((b, l, w), BF16),
        name="sb_attn_prompt",
    )(bias, q, k, v)


CONV_TR = 32
CONV_PAD = 32


def _layernorm_silu(y, lnw, lnb):
    mu = jnp.mean(y, axis=-1, keepdims=True)
    yc = y - mu
    var = jnp.mean(yc * yc, axis=-1, keepdims=True)
    return _silu(yc * lax.rsqrt(var + LN_EPS) * lnw + lnb)


def _conv_kernel(u_ref, w_ref, b_ref, lnw_ref, lnb_ref, o_ref, full_ref):
    l = u_ref.shape[1]
    tr = CONV_TR
    full_ref[0:CONV_PAD, :] = jnp.zeros((CONV_PAD, CONV_WIDTH), F32)
    full_ref[CONV_PAD:CONV_PAD + l, :] = u_ref[0]
    full_ref[CONV_PAD + l:, :] = jnp.zeros((SUBLANES, CONV_WIDTH), F32)
    first = CONV_PAD - (CONV_K - 1)
    bias = b_ref[...]
    lnw = lnw_ref[...]
    lnb = lnb_ref[...]

    def tile(t, c):
        r0 = pl.multiple_of(t * tr, tr)
        y = jnp.zeros((tr, CONV_WIDTH), F32)
        for r in range(SUBLANES):
            g = None
            for m in range(first, first + CONV_K):
                if m % SUBLANES != r:
                    continue
                win = full_ref[pl.ds(r0 + (m - r), tr + SUBLANES), :]
                term = w_ref[m - first:m - first + 1, :] * win
                g = term if g is None else g + term
            if g is not None:
                y = y + g[r:r + tr, :]
        o_ref[0, pl.ds(r0, tr), :] = _layernorm_silu(y + bias, lnw, lnb).astype(o_ref.dtype)
        return c

    lax.fori_loop(0, l // tr, tile, 0)


def conv_prompt(u, conv_w, conv_b, ln_w, ln_b):
    b, l, c = u.shape
    vec = lambda i: (0, 0)
    return pl.pallas_call(
        _conv_kernel,
        grid=(b,),
        in_specs=[pl.BlockSpec((1, l, c), lambda i: (i, 0, 0)),
                  pl.BlockSpec((CONV_K, c), vec),
                  pl.BlockSpec((1, c), vec), pl.BlockSpec((1, c), vec), pl.BlockSpec((1, c), vec)],
        out_specs=pl.BlockSpec((1, l, c), lambda i: (i, 0, 0)),
        out_shape=jax.ShapeDtypeStruct((b, l, c), BF16),
        scratch_shapes=[pltpu.VMEM((CONV_PAD + l + SUBLANES, c), F32)],
        name="conv_prompt",
    )(u, conv_w, conv_b.reshape(1, c), ln_w.reshape(1, c), ln_b.reshape(1, c))


def _outproj_kernel(x_ref, oa_ref, oc_ref, w_ref, g_ref, o_ref):
    m = _mm(oa_ref[0], w_ref[:SB_WIDTH, :]) + _mm(oc_ref[0], w_ref[SB_WIDTH:, :])
    o_ref[0] = x_ref[0] + g_ref[0] * m


def outproj(x, oa, oc, w_out, gate, tm):
    b, l, d = x.shape
    row = lambda i, j: (i, j, 0)
    return pl.pallas_call(
        _outproj_kernel,
        grid=(b, l // tm),
        in_specs=[pl.BlockSpec((1, tm, d), row),
                  pl.BlockSpec((1, tm, SB_WIDTH), row),
                  pl.BlockSpec((1, tm, CONV_WIDTH), row),
                  pl.BlockSpec(w_out.shape, lambda i, j: (0, 0)),
                  _mod_spec(gate, tm)],
        out_specs=pl.BlockSpec((1, tm, d), row),
        out_shape=jax.ShapeDtypeStruct((b, l, d), F32),
        name="outproj",
    )(x, oa, oc, w_out, gate)


def _normmod_kernel(x_ref, nw_ref, sh_ref, sc_ref, o_ref):
    o_ref[0] = _rms_mod(x_ref[0], nw_ref[...], sh_ref[0], sc_ref[0]).astype(o_ref.dtype)


def normmod(x, nw, shift, scale, tm, dtype):
    b, l, d = x.shape
    row = lambda i, j: (i, j, 0)
    return pl.pallas_call(
        _normmod_kernel,
        grid=(b, l // tm),
        in_specs=[pl.BlockSpec((1, tm, d), row), pl.BlockSpec((1, d), lambda i, j: (0, 0)),
                  _mod_spec(shift, tm), _mod_spec(scale, tm)],
        out_specs=pl.BlockSpec((1, tm, d), row),
        out_shape=jax.ShapeDtypeStruct((b, l, d), dtype),
        name="normmod",
    )(x, nw.reshape(1, d), shift, scale)


def _cmul(ar, ai, br, bi):
    return ar * br - ai * bi, ar * bi + ai * br


def s5_discretize(lam_re, lam_im, b_re, b_im, log_dt):
    dt = jnp.exp(log_dt)[:, None]
    mag = jnp.exp(lam_re * dt)
    ar, ai = mag * jnp.cos(lam_im * dt), mag * jnp.sin(lam_im * dt)
    den = lam_re * lam_re + lam_im * lam_im
    fr = ((ar - 1) * lam_re + ai * lam_im) / den
    fi = (ai * lam_re - (ar - 1) * lam_im) / den
    bbr, bbi = _cmul(fr[..., None], fi[..., None], b_re, b_im)
    return ar, ai, bbr, bbi


def s5_chunk_operators(ar, ai, bbr, bbi, c_re, c_im, d):
    g, p, c = bbr.shape
    ch = S5_CHUNK
    pr, pi = [jnp.ones_like(ar)], [jnp.zeros_like(ai)]
    for _ in range(ch):
        nr, ni = _cmul(pr[-1], pi[-1], ar, ai)
        pr.append(nr)
        pi.append(ni)
    pr, pi = jnp.stack(pr), jnp.stack(pi)
    car, cai = _cmul(c_re[None], c_im[None], pr[:, :, None, :], pi[:, :, None, :])
    ein = functools.partial(jnp.einsum, precision=HIGHEST)
    ktau = ein('tgcp,gpd->tgcd', car[:ch], bbr) - ein('tgcp,gpd->tgcd', cai[:ch], bbi)
    ktau = ktau.at[0].add(jnp.eye(c, dtype=F32)[None] * d.reshape(g, c)[:, :, None])
    s_idx = jnp.arange(ch)
    tau = s_idx[None, :] - s_idx[:, None]
    kbig = jnp.where((tau >= 0)[:, :, None, None, None], ktau[jnp.clip(tau, 0, ch - 1)], 0.0)
    kbig = kbig.transpose(2, 0, 4, 1, 3).reshape(g, ch * c, ch * c)
    wr, wi = _cmul(pr[ch - 1 - s_idx][..., None], pi[ch - 1 - s_idx][..., None], bbr[None], bbi[None])
    bbig = jnp.concatenate([wr.transpose(1, 0, 3, 2).reshape(g, ch * c, p),
                            wi.transpose(1, 0, 3, 2).reshape(g, ch * c, p)], axis=-1)
    cre = car[1:].transpose(1, 3, 0, 2).reshape(g, p, ch * c)
    cim = cai[1:].transpose(1, 3, 0, 2).reshape(g, p, ch * c)
    cbig = jnp.concatenate([cre, -cim], axis=1)
    kb = jnp.concatenate([kbig, bbig], axis=-1)
    a_chunk = jnp.stack([jnp.concatenate([pr[ch], pr[ch]], -1),
                         jnp.concatenate([-pi[ch], pi[ch]], -1)], axis=1)
    return kb.astype(BF16), cbig.astype(BF16), a_chunk


def _s5_kernel(u_ref, kb_ref, cb_ref, a_ref, y_ref, hl_ref, s_scr, hp_scr):
    nb = hl_ref.shape[1]
    width = y_ref.shape[2]
    u = u_ref[0]
    kb = kb_ref[0]
    y_local = _dot(u, kb[:, :width])
    s_scr[...] = _dot(u, kb[:, width:])
    a_same = a_ref[0, 0:1, :]
    a_swap = a_ref[0, 1:2, :]
    p = a_same.shape[-1] // 2

    def step(k, h):
        r = pl.multiple_of(k * nb, nb)
        hp_scr[pl.ds(r, nb), :] = h
        return h * a_same + pltpu.roll(h, p, 1) * a_swap + s_scr[pl.ds(r, nb), :]

    h_last = lax.fori_loop(0, u.shape[0] // nb, step, jnp.zeros((nb, 2 * p), F32))
    hl_ref[0] = h_last
    y = y_local + _dot(hp_scr[...].astype(BF16), cb_ref[0])
    y_ref[0] = jax.nn.gelu(y).astype(y_ref.dtype)


def s5_prompt(u, kb, cb, a_chunk):
    b, l, dm = u.shape
    g = kb.shape[0]
    c = dm // g
    ch = S5_CHUNK
    nk = l // ch
    p2 = cb.shape[1]
    ug = u.reshape(b, nk, ch, g, c).transpose(3, 1, 0, 2, 4).reshape(g, nk * b, ch * c)
    grp = lambda i: (i, 0, 0)
    y, hl = pl.pallas_call(
        _s5_kernel,
        grid=(g,),
        in_specs=[pl.BlockSpec((1, nk * b, ch * c), grp),
                  pl.BlockSpec((1,) + kb.shape[1:], grp),
                  pl.BlockSpec((1,) + cb.shape[1:], grp),
                  pl.BlockSpec((1, 2, p2), grp)],
        out_specs=[pl.BlockSpec((1, nk * b, ch * c), grp), pl.BlockSpec((1, b, p2), grp)],
        out_shape=[jax.ShapeDtypeStruct((g, nk * b, ch * c), BF16), jax.ShapeDtypeStruct((g, b, p2), F32)],
        scratch_shapes=[pltpu.VMEM((nk * b, p2), F32), pltpu.VMEM((nk * b, p2), F32)],
        name="s5_prompt",
    )(ug, kb, cb, a_chunk)
    y = y.reshape(g, nk, b, ch, c).transpose(2, 1, 3, 0, 4).reshape(b, l, dm)
    return y, hl.transpose(1, 0, 2)


def _glu_kernel(x_ref, y_ref, w_ref, g_ref, o_ref):
    z = _mm(y_ref[0], w_ref[...])
    d = o_ref.shape[-1]
    o_ref[0] = x_ref[0] + g_ref[0] * (z[:, :d] * jax.nn.sigmoid(z[:, d:]))


def glu_residual(x, y, w_glu, gate, tm):
    b, l, d = x.shape
    row = lambda i, j: (i, j, 0)
    return pl.pallas_call(
        _glu_kernel,
        grid=(b, l // tm),
        in_specs=[pl.BlockSpec((1, tm, d), row), pl.BlockSpec((1, tm, d), row),
                  pl.BlockSpec(w_glu.shape, lambda i, j: (0, 0)),
                  _mod_spec(gate, tm)],
        out_specs=pl.BlockSpec((1, tm, d), row),
        out_shape=jax.ShapeDtypeStruct((b, l, d), F32),
        name="glu_residual",
    )(x, y, w_glu, gate)


def _route(logits_t, b_col):
    e_tot = logits_t.shape[0]
    epg = EXPERTS_PER_GROUP
    ng = e_tot // epg
    scores = jax.nn.sigmoid(logits_t)
    sel_all = scores + b_col
    sel = [sel_all[e:e + 1, :] for e in range(e_tot)]
    sc = [scores[e:e + 1, :] for e in range(e_tot)]
    gsum = []
    for g in range(ng):
        a, b, c, d = sel[epg * g:epg * g + epg]
        hi1, lo1, hi2, lo2 = jnp.maximum(a, b), jnp.minimum(a, b), jnp.maximum(c, d), jnp.minimum(c, d)
        gsum.append(jnp.maximum(hi1, hi2) + jnp.maximum(jnp.minimum(hi1, hi2), jnp.maximum(lo1, lo2)))
    best = gsum[0]
    gi = jnp.zeros_like(best, dtype=jnp.int32)
    for g in range(1, ng):
        upd = gsum[g] > best
        gi = jnp.where(upd, g, gi)
        best = jnp.where(upd, gsum[g], best)
    picked = []
    for e in range(e_tot):
        g, j = divmod(e, epg)
        ahead = jnp.zeros_like(gi)
        for j2 in range(epg):
            if j2 == j:
                continue
            o = sel[epg * g + j2]
            before = (o >= sel[e]) if j2 < j else (o > sel[e])
            ahead = ahead + before.astype(jnp.int32)
        picked.append((gi == g) & (ahead < 2))
    wk = [jnp.where(picked[e], sc[e], 0.0) for e in range(e_tot)]
    tot = wk[0]
    for e in range(1, e_tot):
        tot = tot + wk[e]
    return jnp.concatenate([w / tot for w in wk], axis=0)


def _router_kernel(x_ref, nw_ref, sh_ref, sc_ref, wr_ref, br_ref, h_ref, g_ref):
    h = _rms_mod(x_ref[0], nw_ref[...], sh_ref[0], sc_ref[0])
    h_ref[0] = h.astype(h_ref.dtype)
    logits_t = lax.dot_general(wr_ref[...], h, NT_DIMS, preferred_element_type=F32, precision=HIGHEST)
    g_ref[0] = _route(logits_t, br_ref[...])


def router(x, nw, shift, scale, w_router, b_router, tm, h_dtype):
    b, l, d = x.shape
    e = w_router.shape[1]
    row = lambda i, j: (i, j, 0)
    const = lambda i, j: (0, 0)
    return pl.pallas_call(
        _router_kernel,
        grid=(b, l // tm),
        in_specs=[pl.BlockSpec((1, tm, d), row), pl.BlockSpec((1, d), const),
                  _mod_spec(shift, tm), _mod_spec(scale, tm),
                  pl.BlockSpec((e, d), const), pl.BlockSpec((e, 1), const)],
        out_specs=[pl.BlockSpec((1, tm, d), row), pl.BlockSpec((1, e, tm), lambda i, j: (i, 0, j))],
        out_shape=[jax.ShapeDtypeStruct((b, l, d), h_dtype), jax.ShapeDtypeStruct((b, e, l), F32)],
        name="router",
    )(x, nw.reshape(1, d), shift, scale, w_router.T, b_router.reshape(e, 1))


def _moe_dense_kernel(h_ref, gt_ref, wg_ref, wu_ref, wd_ref, x_ref, g6_ref, o_ref, acc_ref):
    e = pl.program_id(2)

    @pl.when(e == 0)
    def _():
        acc_ref[...] = jnp.zeros_like(acc_ref)

    h = h_ref[0]
    hg = _mm(h, wg_ref[0])
    hu = _mm(h, wu_ref[0])
    act = _silu(hg) * hu * gt_ref[0, 0]
    acc_ref[...] += _mm(act, wd_ref[0])

    @pl.when(e == pl.num_programs(2) - 1)
    def _():
        o_ref[0] = x_ref[0] + g6_ref[0] * acc_ref[...]


def moe_dense(h, gates_col, w_gate, w_up, w_down, x, gate6, tm):
    b, l, d = x.shape
    e, _, f = w_gate.shape
    row = lambda i, j, k: (i, j, 0)
    return pl.pallas_call(
        _moe_dense_kernel,
        grid=(b, l // tm, e),
        in_specs=[pl.BlockSpec((1, tm, d), row),
                  pl.BlockSpec((1, 1, tm, 1), lambda i, j, k: (i, k, j, 0)),
                  pl.BlockSpec((1, d, f), lambda i, j, k: (k, 0, 0)),
                  pl.BlockSpec((1, d, f), lambda i, j, k: (k, 0, 0)),
                  pl.BlockSpec((1, f, d), lambda i, j, k: (k, 0, 0)),
                  pl.BlockSpec((1, tm, d), row),
                  _mod_spec(gate6, tm)],
        out_specs=pl.BlockSpec((1, tm, d), row),
        out_shape=jax.ShapeDtypeStruct((b, l, d), F32),
        scratch_shapes=[pltpu.VMEM((tm, d), F32)],
        name="moe_dense",
    )(h, gates_col, w_gate, w_up, w_down, x, gate6)


def moe_block(x, nw, shift, scale, gate6, w_router, b_router, w_gate, w_up, w_down, tm_route, tm_moe, h_dtype):
    b, l, d = x.shape
    h, gates_t = router(x, nw, shift, scale, w_router, b_router, tm_route, h_dtype)
    return moe_dense(h, gates_t.reshape(b, -1, l, 1), w_gate, w_up, w_down, x, gate6, tm_moe)


def _final_kernel(x_ref, nw_ref, o_ref):
    x = x_ref[0]
    o_ref[0] = x * lax.rsqrt(jnp.mean(x * x, axis=-1, keepdims=True) + EPS) * nw_ref[...]


def final_norm(x, nw, tm):
    b, l, d = x.shape
    row = lambda i, j: (i, j, 0)
    return pl.pallas_call(
        _final_kernel,
        grid=(b, l // tm),
        in_specs=[pl.BlockSpec((1, tm, d), row), pl.BlockSpec((1, d), lambda i, j: (0, 0))],
        out_specs=pl.BlockSpec((1, tm, d), row),
        out_shape=jax.ShapeDtypeStruct((b, l, d), F32),
        name="final_norm",
    )(x, nw.reshape(1, d))


PAGES_PER_STEP = 8


def _sb_attn_sample_kernel(pt_ref, bias_ref, q_ref, *rest):
    npp = PAGES_PER_STEP
    k_refs, v_refs = rest[:npp], rest[npp:2 * npp]
    o_ref, acc_ref, car_ref = rest[2 * npp:]
    s = pl.program_id(1)
    nh = SB_HEADS

    @pl.when(s == 0)
    def _():
        acc_ref[...] = jnp.zeros_like(acc_ref)
        car_ref[...] = jnp.zeros_like(car_ref)

    row = lax.broadcasted_iota(jnp.int32, (nh, SB_WIDTH), 0)
    lane_head = lax.broadcasted_iota(jnp.int32, (nh, SB_WIDTH), 1) // SB_HEAD_DIM
    qrows = jnp.where(row == lane_head, jnp.broadcast_to(q_ref[0], (nh, SB_WIDTH)), 0.0)
    q_hi, q_lo = _split_bf16(qrows)
    q_both = jnp.concatenate([q_hi, q_lo], axis=0)
    bias_col = bias_ref[...]
    pg = k_refs[0].shape[0]
    lane = lax.broadcasted_iota(jnp.int32, (nh, pg), 1)

    for r in range(npp):
        k_hi, k_lo = _split_bf16(k_refs[r][...])
        zz = lax.dot_general(q_both, k_hi, NT_DIMS, preferred_element_type=F32)
        z = zz[:nh] + zz[nh:] + lax.dot_general(q_hi, k_lo, NT_DIMS, preferred_element_type=F32) + bias_col
        sp = _softplus(z)
        l1m = -sp
        c = l1m
        d = 1
        while d < pg:
            c = c + jnp.where(lane + d < pg, pltpu.roll(c, pg - d, 1), 0.0)
            d *= 2
        w = jnp.exp(z - sp + car_ref[...] + c - l1m)
        car_ref[...] += jnp.broadcast_to(c[:, 0:1], (nh, pg))
        w_hi, w_lo = _split_bf16(w)
        v_hi, v_lo = _split_bf16(v_refs[r][...])
        oo = _dot(jnp.concatenate([w_hi, w_lo], axis=0), v_hi)
        acc_ref[...] += oo[:nh] + oo[nh:] + _dot(w_hi, v_lo)

    @pl.when(s == pl.num_programs(1) - 1)
    def _():
        o_ref[0] = jnp.sum(jnp.where(row == lane_head, acc_ref[...], 0.0), axis=0, keepdims=True)


def sb_attn_sample(q, cache_k, cache_v, layer, page_table, bias):
    n, w = q.shape
    n_pages = page_table.shape[1]
    pg = cache_k.shape[2]
    npp = PAGES_PER_STEP
    nsteps = n_pages // npp

    def page_spec(r):
        def imap(b, s, pt):
            return (layer, pt[b * n_pages + (n_pages - 1 - (s * npp + r))], 0, 0)
        return pl.BlockSpec((None, None, pg, w), imap)

    grid_spec = pltpu.PrefetchScalarGridSpec(
        num_scalar_prefetch=1,
        grid=(n, nsteps),
        in_specs=([pl.BlockSpec((SB_HEADS, 1), lambda b, s, pt: (0, 0)),
                   pl.BlockSpec((1, 1, w), lambda b, s, pt: (b, 0, 0))]
                  + [page_spec(r) for r in range(npp)] * 2),
        out_specs=pl.BlockSpec((1, 1, w), lambda b, s, pt: (b, 0, 0)),
        scratch_shapes=[pltpu.VMEM((SB_HEADS, w), F32), pltpu.VMEM((SB_HEADS, pg), F32)],
    )
    out = pl.pallas_call(
        _sb_attn_sample_kernel,
        grid_spec=grid_spec,
        out_shape=jax.ShapeDtypeStruct((n, 1, w), F32),
        name="sb_attn_sample",
    )(page_table.reshape(-1), bias.reshape(SB_HEADS, 1), q.reshape(n, 1, w),
      *([cache_k] * npp), *([cache_v] * npp))
    return out.reshape(n, w)


def _conv_sample_kernel(u_ref, st_ref, w_ref, b_ref, lnw_ref, lnb_ref, o_ref):
    km1 = CONV_K - 1
    w = w_ref[...]
    y = jnp.sum(st_ref[...] * w[None, :km1, :], axis=1) + u_ref[...] * w[km1:, :] + b_ref[...]
    o_ref[...] = _layernorm_silu(y, lnw_ref[...], lnb_ref[...])


def conv_sample(u, state, conv_w, conv_b, ln_w, ln_b):
    n, c = u.shape
    return pl.pallas_call(
        _conv_sample_kernel,
        out_shape=jax.ShapeDtypeStruct((n, c), F32),
        name="conv_sample",
    )(u, state, conv_w, conv_b.reshape(1, c), ln_w.reshape(1, c), ln_b.reshape(1, c))


def _s5_sample_kernel(u_ref, h0r_ref, h0i_ref, ar_ref, ai_ref, bbr_ref, bbi_ref, cr_ref, ci_ref, d_ref,
                      y_ref, hr_ref, hi_ref):
    u = u_ref[...]
    ein = functools.partial(jnp.einsum, precision=HIGHEST, preferred_element_type=F32)
    hr0, hi0 = h0r_ref[...], h0i_ref[...]
    ar, ai = ar_ref[...], ai_ref[...]
    hr = ar * hr0 - ai * hi0 + ein('gnc,gcp->gnp', u, bbr_ref[...])
    hi = ar * hi0 + ai * hr0 + ein('gnc,gcp->gnp', u, bbi_ref[...])
    hr_ref[...] = hr
    hi_ref[...] = hi
    y = ein('gnp,gpc->gnc', hr, cr_ref[...]) - ein('gnp,gpc->gnc', hi, ci_ref[...]) + d_ref[...] * u
    y_ref[...] = jax.nn.gelu(y)


def s5_sample(u, h0_re, h0_im, ar, ai, bbr, bbi, c_re, c_im, d):
    n, dm = u.shape
    g, p, c = bbr.shape
    t = lambda a: a.transpose(1, 0, 2)
    sh = jax.ShapeDtypeStruct
    y, hr, hi = pl.pallas_call(
        _s5_sample_kernel,
        out_shape=[sh((g, n, c), F32), sh((g, n, p), F32), sh((g, n, p), F32)],
        name="s5_sample",
    )(t(u.reshape(n, g, c)), t(h0_re), t(h0_im), ar[:, None, :], ai[:, None, :],
      bbr.transpose(0, 2, 1), bbi.transpose(0, 2, 1), c_re.transpose(0, 2, 1), c_im.transpose(0, 2, 1),
      d.reshape(g, 1, c))
    return t(y).reshape(n, dm), t(hr), t(hi)


def kernel(x_prompt, x_sample, c_prompt, c_sample, cache_k, cache_v, page_table, state_conv, state_ssm_re, state_ssm_im, w_ada, b_ada, norm_w, final_norm_w, w_in_mix, w_out_mix, sb_bias, conv_w, conv_b, conv_ln_w, conv_ln_b, ssm_lambda_re, ssm_lambda_im, ssm_b_re, ssm_b_im, ssm_c_re, ssm_c_im, ssm_d, ssm_log_dt, w_glu, w_router, b_router, w_gate, w_up, w_down):
    nb, l, d = x_prompt.shape
    ns = x_sample.shape[0]
    depth = w_ada.shape[0]
    n_attn = cache_k.shape[0]
    assert nb == SUBLANES and x_sample.shape[1] == 1
    tm = 512

    ada = ada_all(jnp.concatenate([c_prompt, c_sample], axis=0), w_ada, b_ada)
    ada = ada.reshape(depth, nb + ns, N_ADA, d)
    cache_k2 = cache_k.reshape(cache_k.shape[:3] + (SB_WIDTH,))
    cache_v2 = cache_v.reshape(cache_v.shape[:3] + (SB_WIDTH,))

    xp = x_prompt
    xs = x_sample.reshape(1, ns, d)
    kp_l, vp_l, ks_l, vs_l, cvp_l, cvs_l = [], [], [], [], [], []
    srp_l, sip_l, srs_l, sis_l = [], [], [], []
    for layer in range(depth):
        ap = [ada[layer, :nb, j].reshape(nb, 1, d) for j in range(N_ADA)]
        asm = [ada[layer, nb:, j].reshape(1, ns, d) for j in range(N_ADA)]
        nw = norm_w[layer]
        i = layer // 2
        if layer % 2 == 0:
            w_in_b, w_out_b = w_in_mix[i].astype(BF16), w_out_mix[i].astype(BF16)
            q, k, v, u = inproj(xp, nw[0], ap[0], ap[1], w_in_b, tm, BF16)
            oa = sb_attn_prompt(q, k, v, sb_bias[i])
            oc = conv_prompt(u, conv_w[i], conv_b[i], conv_ln_w[i], conv_ln_b[i])
            xp = outproj(xp, oa, oc, w_out_b, ap[2], tm)
            kp_l.append(k.reshape(nb, l, SB_HEADS, SB_HEAD_DIM))
            vp_l.append(v.reshape(nb, l, SB_HEADS, SB_HEAD_DIM))
            cvp_l.append(u[:, l - (CONV_K - 1):, :])

            qs, ks_, vs_, us = inproj(xs, nw[0], asm[0], asm[1], w_in_mix[i], ns, F32)
            oas = sb_attn_sample(qs[0], cache_k2, cache_v2, i, page_table, sb_bias[i])
            ocs = conv_sample(us[0], state_conv[i], conv_w[i], conv_b[i], conv_ln_w[i], conv_ln_b[i])
            xs = outproj(xs, oas[None], ocs[None], w_out_mix[i], asm[2], ns)
            ks_l.append(ks_.reshape(ns, 1, SB_HEADS, SB_HEAD_DIM))
            vs_l.append(vs_.reshape(ns, 1, SB_HEADS, SB_HEAD_DIM))
            cvs_l.append(jnp.concatenate([state_conv[i][:, 1:, :], us.reshape(ns, 1, CONV_WIDTH)], axis=1))
        else:
            ar, ai, bbr, bbi = s5_discretize(ssm_lambda_re[i], ssm_lambda_im[i], ssm_b_re[i], ssm_b_im[i],
                                             ssm_log_dt[i])
            kb, cb, a_chunk = s5_chunk_operators(ar, ai, bbr, bbi, ssm_c_re[i], ssm_c_im[i], ssm_d[i])
            hp = normmod(xp, nw[0], ap[0], ap[1], tm, BF16)
            yp, hl = s5_prompt(hp, kb, cb, a_chunk)
            xp = glu_residual(xp, yp, w_glu[i].astype(BF16), ap[2], tm)
            p = hl.shape[-1] // 2
            srp_l.append(hl[..., :p])
            sip_l.append(hl[..., p:])

            hs = normmod(xs, nw[0], asm[0], asm[1], ns, F32)
            ys, srs, sis = s5_sample(hs[0], state_ssm_re[i], state_ssm_im[i], ar, ai, bbr, bbi,
                                     ssm_c_re[i], ssm_c_im[i], ssm_d[i])
            xs = glu_residual(xs, ys[None], w_glu[i], asm[2], ns)
            srs_l.append(srs)
            sis_l.append(sis)
        xp = moe_block(xp, nw[1], ap[3], ap[4], ap[5], w_router, b_router,
                       w_gate[layer].astype(BF16), w_up[layer].astype(BF16), w_down[layer].astype(BF16),
                       min(l, 1024), min(l, 1024), BF16)
        xs = moe_block(xs, nw[1], asm[3], asm[4], asm[5], w_router, b_router,
                       w_gate[layer], w_up[layer], w_down[layer], ns, ns, F32)
    y_prompt = final_norm(xp, final_norm_w, tm)
    y_sample = final_norm(xs, final_norm_w, ns).reshape(ns, 1, d)
    return (y_prompt, y_sample,
            jnp.stack(kp_l), jnp.stack(vp_l), jnp.stack(ks_l), jnp.stack(vs_l),
            jnp.stack(cvp_l), jnp.stack(cvs_l),
            jnp.stack(srp_l), jnp.stack(sip_l), jnp.stack(srs_l), jnp.stack(sis_l))
```

```python
import functools
import math

import jax
import jax.numpy as jnp
from jax import lax
from jax.experimental import pallas as pl
from jax.experimental.pallas import tpu as pltpu

F32 = jnp.float32
BF16 = jnp.bfloat16
HIGHEST = lax.Precision.HIGHEST

EPS = 1e-6
LN_EPS = 1e-5
SB_HEADS = 8
SB_HEAD_DIM = 64
SB_WIDTH = SB_HEADS * SB_HEAD_DIM
CONV_WIDTH = 512
CONV_K = 31
SSM_GROUP = 16
SSM_STATE = 64
N_EXPERTS = 16
EXPERTS_PER_GROUP = 4
N_ADA = 6

LANES = 128
SUBLANES = 8
S5_CHUNK = SUBLANES
NT_DIMS = (((1,), (1,)), ((), ()))


def _dot(a, b):
    return jnp.dot(a, b, preferred_element_type=F32)


def _dotf(a, b):
    return jnp.dot(a, b, preferred_element_type=F32, precision=HIGHEST)


def _mm(a, w):
    if w.dtype == BF16:
        return _dot(a.astype(BF16), w)
    return _dotf(a, w)


def _silu(x):
    return x * jax.nn.sigmoid(x)


def _rms_mod(x, nw, shift, scale):
    y = x * lax.rsqrt(jnp.mean(x * x, axis=-1, keepdims=True) + EPS) * nw
    return y * (1 + scale) + shift


def _mod_spec(a, tm):
    d = a.shape[-1]
    if a.shape[1] == 1:
        return pl.BlockSpec((1, 1, d), lambda i, j, *_: (i, 0, 0))
    return pl.BlockSpec((1, tm, d), lambda i, j, *_: (i, j, 0))


def _split_bf16(x):
    hi = x.astype(BF16)
    lo = (x - hi.astype(F32)).astype(BF16)
    return hi, lo


def _ada_kernel(c_ref, w_ref, b_ref, o_ref):
    o_ref[0] = _dotf(_silu(c_ref[...]), w_ref[0]) + b_ref[0]


def ada_all(c, w_ada, b_ada):
    depth, d, n = w_ada.shape
    r = c.shape[0]
    tn = 1024
    return pl.pallas_call(
        _ada_kernel,
        grid=(depth, n // tn),
        in_specs=[pl.BlockSpec((r, d), lambda l, j: (0, 0)),
                  pl.BlockSpec((1, d, tn), lambda l, j: (l, 0, j)),
                  pl.BlockSpec((1, 1, tn), lambda l, j: (l, 0, j))],
        out_specs=pl.BlockSpec((1, r, tn), lambda l, j: (l, 0, j)),
        out_shape=jax.ShapeDtypeStruct((depth, r, n), F32),
        name="ada",
    )(c, w_ada, b_ada.reshape(depth, 1, n))


def _inproj_kernel(x_ref, nw_ref, sh_ref, sc_ref, w_ref, q_ref, k_ref, v_ref, u_ref):
    h = _rms_mod(x_ref[0], nw_ref[...], sh_ref[0], sc_ref[0])
    p = _mm(h, w_ref[...])
    w = SB_WIDTH
    q_ref[0] = (p[:, :w] * (1.0 / math.sqrt(SB_HEAD_DIM))).astype(q_ref.dtype)
    k_ref[0] = p[:, w:2 * w]
    v_ref[0] = p[:, 2 * w:3 * w]
    a = p[:, 3 * w:3 * w + CONV_WIDTH]
    g = p[:, 3 * w + CONV_WIDTH:]
    u_ref[0] = a * jax.nn.sigmoid(g)


def inproj(x, nw, shift, scale, w_in, tm, q_dtype):
    b, l, d = x.shape
    n = w_in.shape[1]
    row = lambda i, j: (i, j, 0)
    o = jax.ShapeDtypeStruct((b, l, SB_WIDTH), F32)
    return pl.pallas_call(
        _inproj_kernel,
        grid=(b, l // tm),
        in_specs=[pl.BlockSpec((1, tm, d), row),
                  pl.BlockSpec((1, d), lambda i, j: (0, 0)),
                  _mod_spec(shift, tm), _mod_spec(scale, tm),
                  pl.BlockSpec((d, n), lambda i, j: (0, 0))],
        out_specs=[pl.BlockSpec((1, tm, SB_WIDTH), row)] * 4,
        out_shape=[jax.ShapeDtypeStruct((b, l, SB_WIDTH), q_dtype), o, o, o],
        name="inproj",
    )(x, nw.reshape(1, d), shift, scale, w_in)


ATT_T = 512
ATT_SUB = LANES


def _softplus(z):
    return jnp.maximum(z, 0.0) + jnp.log(1.0 + jnp.exp(-jnp.abs(z)))


def _sb_attn_kernel(bias_ref, q_ref, k_ref, v_ref, o_ref, acc_ref, run_ref):
    t, sub = ATT_T, ATT_SUB
    hp = pl.program_id(1)
    i = pl.program_id(2)
    q2 = q_ref[0]
    lane = lax.broadcasted_iota(jnp.int32, (t, LANES), 1)
    head_of_lane = lane // SB_HEAD_DIM
    qm = [jnp.where(head_of_lane == hh, q2, jnp.zeros_like(q2)) for hh in range(2)]
    bias = [bias_ref[2 * hp + hh] for hh in range(2)]
    r_io = lax.broadcasted_iota(jnp.int32, (sub, 2 * sub), 0)
    c_io = lax.broadcasted_iota(jnp.int32, (sub, 2 * sub), 1)
    tri_ones = ((r_io > c_io) | (c_io >= sub)).astype(BF16)
    acc_ref[...] = jnp.zeros_like(acc_ref)
    run_ref[...] = jnp.zeros_like(run_ref)
    causal = lax.broadcasted_iota(jnp.int32, (t, t), 1) < lax.broadcasted_iota(jnp.int32, (t, t), 0)

    def chunk(j, masked):
        start = pl.multiple_of(j * t, t)
        kb = k_ref[0, pl.ds(start, t), :].astype(BF16)
        vb = v_ref[0, pl.ds(start, t), :].astype(BF16)
        for hh in range(2):
            z = lax.dot_general(qm[hh], kb, NT_DIMS, preferred_element_type=F32) + bias[hh]
            sp = _softplus(z)
            if masked:
                sp = jnp.where(causal, sp, 0.0)
            spb = sp.astype(BF16)
            run = run_ref[hh]
            ws = []
            for blk in range(t // sub - 1, -1, -1):
                cols = slice(blk * sub, (blk + 1) * sub)
                sums = _dot(spb[:, cols], tri_ones)
                w = jnp.exp(z[:, cols] - sp[:, cols] - (run + sums[:, :sub]))
                if masked:
                    w = jnp.where(causal[:, cols], w, 0.0)
                ws.append(w.astype(BF16))
                run = run + sums[:, sub:]
            run_ref[hh] = run
            acc_ref[hh] += _dot(jnp.concatenate(ws[::-1], axis=1), vb)

    chunk(i, True)

    def body(n, c):
        chunk(i - 1 - n, False)
        return c

    lax.fori_loop(0, i, body, 0)
    o_ref[0] = jnp.where(head_of_lane == 0, acc_ref[0], acc_ref[1]).astype(o_ref.dtype)


def sb_attn_prompt(q, k, v, bias):
    b, l, w = q.shape
    t = ATT_T
    return pl.pallas_call(
        _sb_attn_kernel,
        grid=(b, w // LANES, l // t),
        in_specs=[pl.BlockSpec(memory_space=pltpu.SMEM),
                  pl.BlockSpec((1, t, LANES), lambda bi, hp, i: (bi, i, hp)),
                  pl.BlockSpec((1, l, LANES), lambda bi, hp, i: (bi, 0, hp)),
                  pl.BlockSpec((1, l, LANES), lambda bi, hp, i: (bi, 0, hp))],
        out_specs=pl.BlockSpec((1, t, LANES), lambda bi, hp, i: (bi, i, hp)),
        out_shape=jax.ShapeDtypeStruct((b, l, w), BF16),
        scratch_shapes=[pltpu.VMEM((2, t, LANES), F32), pltpu.VMEM((2, t, ATT_SUB), F32)],
        name="sb_attn_prompt",
    )(bias, q, k, v)


CONV_TR = 32
CONV_PAD = 32


def _layernorm_silu(y, lnw, lnb):
    mu = jnp.mean(y, axis=-1, keepdims=True)
    yc = y - mu
    var = jnp.mean(yc * yc, axis=-1, keepdims=True)
    return _silu(yc * lax.rsqrt(var + LN_EPS) * lnw + lnb)


def _conv_kernel(u_ref, w_ref, b_ref, lnw_ref, lnb_ref, o_ref, full_ref):
    l = u_ref.shape[1]
    tr = CONV_TR
    full_ref[0:CONV_PAD, :] = jnp.zeros((CONV_PAD, CONV_WIDTH), F32)
    full_ref[CONV_PAD:CONV_PAD + l, :] = u_ref[0]
    full_ref[CONV_PAD + l:, :] = jnp.zeros((SUBLANES, CONV_WIDTH), F32)
    first = CONV_PAD - (CONV_K - 1)
    bias = b_ref[...]
    lnw = lnw_ref[...]
    lnb = lnb_ref[...]

    def tile(t, c):
        r0 = pl.multiple_of(t * tr, tr)
        y = jnp.zeros((tr, CONV_WIDTH), F32)
        for r in range(SUBLANES):
            g = None
            for m in range(first, first + CONV_K):
                if m % SUBLANES != r:
                    continue
                win = full_ref[pl.ds(r0 + (m - r), tr + SUBLANES), :]
                term = w_ref[m - first:m - first + 1, :] * win
                g = term if g is None else g + term
            if g is not None:
                y = y + g[r:r + tr, :]
        o_ref[0, pl.ds(r0, tr), :] = _layernorm_silu(y + bias, lnw, lnb).astype(o_ref.dtype)
        return c

    lax.fori_loop(0, l // tr, tile, 0)


def conv_prompt(u, conv_w, conv_b, ln_w, ln_b):
    b, l, c = u.shape
    vec = lambda i: (0, 0)
    return pl.pallas_call(
        _conv_kernel,
        grid=(b,),
        in_specs=[pl.BlockSpec((1, l, c), lambda i: (i, 0, 0)),
                  pl.BlockSpec((CONV_K, c), vec),
                  pl.BlockSpec((1, c), vec), pl.BlockSpec((1, c), vec), pl.BlockSpec((1, c), vec)],
        out_specs=pl.BlockSpec((1, l, c), lambda i: (i, 0, 0)),
        out_shape=jax.ShapeDtypeStruct((b, l, c), BF16),
        scratch_shapes=[pltpu.VMEM((CONV_PAD + l + SUBLANES, c), F32)],
        name="conv_prompt",
    )(u, conv_w, conv_b.reshape(1, c), ln_w.reshape(1, c), ln_b.reshape(1, c))


def _outproj_kernel(x_ref, oa_ref, oc_ref, w_ref, g_ref, o_ref):
    m = _mm(oa_ref[0], w_ref[:SB_WIDTH, :]) + _mm(oc_ref[0], w_ref[SB_WIDTH:, :])
    o_ref[0] = x_ref[0] + g_ref[0] * m


def outproj(x, oa, oc, w_out, gate, tm):
    b, l, d = x.shape
    row = lambda i, j: (i, j, 0)
    return pl.pallas_call(
        _outproj_kernel,
        grid=(b, l // tm),
        in_specs=[pl.BlockSpec((1, tm, d), row),
                  pl.BlockSpec((1, tm, SB_WIDTH), row),
                  pl.BlockSpec((1, tm, CONV_WIDTH), row),
                  pl.BlockSpec(w_out.shape, lambda i, j: (0, 0)),
                  _mod_spec(gate, tm)],
        out_specs=pl.BlockSpec((1, tm, d), row),
        out_shape=jax.ShapeDtypeStruct((b, l, d), F32),
        name="outproj",
    )(x, oa, oc, w_out, gate)


def _normmod_kernel(x_ref, nw_ref, sh_ref, sc_ref, o_ref):
    o_ref[0] = _rms_mod(x_ref[0], nw_ref[...], sh_ref[0], sc_ref[0]).astype(o_ref.dtype)


def normmod(x, nw, shift, scale, tm, dtype):
    b, l, d = x.shape
    row = lambda i, j: (i, j, 0)
    return pl.pallas_call(
        _normmod_kernel,
        grid=(b, l // tm),
        in_specs=[pl.BlockSpec((1, tm, d), row), pl.BlockSpec((1, d), lambda i, j: (0, 0)),
                  _mod_spec(shift, tm), _mod_spec(scale, tm)],
        out_specs=pl.BlockSpec((1, tm, d), row),
        out_shape=jax.ShapeDtypeStruct((b, l, d), dtype),
        name="normmod",
    )(x, nw.reshape(1, d), shift, scale)


def _cmul(ar, ai, br, bi):
    return ar * br - ai * bi, ar * bi + ai * br


def s5_discretize(lam_re, lam_im, b_re, b_im, log_dt):
    dt = jnp.exp(log_dt)[:, None]
    mag = jnp.exp(lam_re * dt)
    ar, ai = mag * jnp.cos(lam_im * dt), mag * jnp.sin(lam_im * dt)
    den = lam_re * lam_re + lam_im * lam_im
    fr = ((ar - 1) * lam_re + ai * lam_im) / den
    fi = (ai * lam_re - (ar - 1) * lam_im) / den
    bbr, bbi = _cmul(fr[..., None], fi[..., None], b_re, b_im)
    return ar, ai, bbr, bbi


def s5_chunk_operators(ar, ai, bbr, bbi, c_re, c_im, d):
    g, p, c = bbr.shape
    ch = S5_CHUNK
    gl = LANES // c
    nt = g // gl
    pr, pi = [jnp.ones_like(ar)], [jnp.zeros_like(ai)]
    for _ in range(ch):
        nr, ni = _cmul(pr[-1], pi[-1], ar, ai)
        pr.append(nr)
        pi.append(ni)
    pr, pi = jnp.stack(pr), jnp.stack(pi)
    car, cai = _cmul(c_re[None], c_im[None], pr[:, :, None, :], pi[:, :, None, :])
    ein = functools.partial(jnp.einsum, precision=HIGHEST)
    ktau = ein('tgcp,gpd->tgcd', car[:ch], bbr) - ein('tgcp,gpd->tgcd', cai[:ch], bbi)
    ktau = ktau.at[0].add(jnp.eye(c, dtype=F32)[None] * d.reshape(g, c)[:, :, None])
    eye = jnp.eye(gl, dtype=F32)
    kblk = (ktau.reshape(ch, nt, gl, c, c).transpose(0, 1, 2, 4, 3)[:, :, :, :, None, :]
            * eye[None, None, :, None, :, None]).reshape(ch, nt, LANES, LANES)
    s_idx = jnp.arange(ch)
    tau = s_idx[None, :] - s_idx[:, None]
    kbig = jnp.where((tau >= 0)[:, :, None, None, None], kblk[jnp.clip(tau, 0, ch - 1)], 0.0)
    kbig = kbig.transpose(2, 0, 3, 1, 4).reshape(nt, ch * LANES, ch * LANES)

    def state_cols(w):
        w = w.reshape(ch, nt, gl, p, c).transpose(0, 1, 2, 4, 3)[:, :, :, :, None, :]
        w = w * eye[None, None, :, None, :, None]
        return w.reshape(ch, nt, LANES, gl * p).transpose(1, 0, 2, 3).reshape(nt, ch * LANES, gl * p)

    rev = ch - 1 - s_idx
    wr, wi = _cmul(pr[rev][..., None], pi[rev][..., None], bbr[None], bbi[None])
    kb = jnp.concatenate([kbig, state_cols(wr), state_cols(wi)], axis=-1)

    def state_rows(w):
        w = w.reshape(ch, nt, gl, c, p).transpose(1, 4, 0, 2, 3)[:, None]
        w = w * eye[None, :, None, None, :, None]
        return w.reshape(nt, gl * p, ch * LANES)

    cb = jnp.concatenate([state_rows(car[1:]), -state_rows(cai[1:])], axis=1)
    pr8, pi8 = pr[ch].reshape(nt, gl * p), pi[ch].reshape(nt, gl * p)
    a_chunk = jnp.stack([jnp.concatenate([pr8, pr8], -1), jnp.concatenate([-pi8, pi8], -1)], axis=1)
    return kb.astype(BF16), cb.astype(BF16), a_chunk


def _s5_kernel(u_ref, kb_ref, cb_ref, a_ref, y_ref, hl_ref, s_scr, y_scr):
    ch = S5_CHUNK
    nk = u_ref.shape[1] // ch
    width = ch * LANES
    ucat = jnp.concatenate([u_ref[0, pl.ds(s, nk, stride=ch), :].astype(BF16) for s in range(ch)], axis=1)
    kb = kb_ref[0]
    s_scr[...] = _dot(ucat, kb[:, width:])
    a_same = a_ref[0, 0:1, :]
    a_swap = a_ref[0, 1:2, :]
    half = a_same.shape[-1] // 2

    def rows(gi, h):
        base = pl.multiple_of(gi * SUBLANES, SUBLANES)
        blk = s_scr[pl.ds(base, SUBLANES), :]
        before = []
        for j in range(SUBLANES):
            before.append(h)
            h = h * a_same + pltpu.roll(h, half, 1) * a_swap + blk[j:j + 1, :]
        s_scr[pl.ds(base, SUBLANES), :] = jnp.concatenate(before, axis=0)
        return h

    hl_ref[0, 0] = lax.fori_loop(0, nk // SUBLANES, rows, jnp.zeros((1, 2 * half), F32))
    y = _dot(ucat, kb[:, :width]) + _dot(s_scr[...].astype(BF16), cb_ref[0])
    for t in range(ch):
        y_scr[pl.ds(t, nk, stride=ch), :] = jax.nn.gelu(y[:, t * LANES:(t + 1) * LANES])
    y_ref[0] = y_scr[...].astype(y_ref.dtype)


def s5_prompt(u, kb, cb, a_chunk):
    b, l, dm = u.shape
    nt = kb.shape[0]
    ch = S5_CHUNK
    sw = cb.shape[1]
    y, hl = pl.pallas_call(
        _s5_kernel,
        grid=(nt, b),
        in_specs=[pl.BlockSpec((1, l, LANES), lambda j, i: (i, 0, j)),
                  pl.BlockSpec((1,) + kb.shape[1:], lambda j, i: (j, 0, 0)),
                  pl.BlockSpec((1,) + cb.shape[1:], lambda j, i: (j, 0, 0)),
                  pl.BlockSpec((1, 2, sw), lambda j, i: (j, 0, 0))],
        out_specs=[pl.BlockSpec((1, l, LANES), lambda j, i: (i, 0, j)),
                   pl.BlockSpec((1, 1, 1, sw), lambda j, i: (i, j, 0, 0))],
        out_shape=[jax.ShapeDtypeStruct((b, l, dm), BF16), jax.ShapeDtypeStruct((b, nt, 1, sw), F32)],
        scratch_shapes=[pltpu.VMEM((l // ch, sw), F32), pltpu.VMEM((l, LANES), F32)],
        name="s5_prompt",
    )(u, kb, cb, a_chunk)
    hl = hl.reshape(b, nt, 2, -1, SSM_STATE)
    return y, hl[:, :, 0].reshape(b, -1, SSM_STATE), hl[:, :, 1].reshape(b, -1, SSM_STATE)


def _glu_kernel(x_ref, y_ref, w_ref, g_ref, o_ref):
    z = _mm(y_ref[0], w_ref[...])
    d = o_ref.shape[-1]
    o_ref[0] = x_ref[0] + g_ref[0] * (z[:, :d] * jax.nn.sigmoid(z[:, d:]))


def glu_residual(x, y, w_glu, gate, tm):
    b, l, d = x.shape
    row = lambda i, j: (i, j, 0)
    return pl.pallas_call(
        _glu_kernel,
        grid=(b, l // tm),
        in_specs=[pl.BlockSpec((1, tm, d), row), pl.BlockSpec((1, tm, d), row),
                  pl.BlockSpec(w_glu.shape, lambda i, j: (0, 0)),
                  _mod_spec(gate, tm)],
        out_specs=pl.BlockSpec((1, tm, d), row),
        out_shape=jax.ShapeDtypeStruct((b, l, d), F32),
        name="glu_residual",
    )(x, y, w_glu, gate)


def _route(logits_t, b_col):
    e_tot = logits_t.shape[0]
    epg = EXPERTS_PER_GROUP
    ng = e_tot // epg
    scores = jax.nn.sigmoid(logits_t)
    sel_all = scores + b_col
    sel = [sel_all[e:e + 1, :] for e in range(e_tot)]
    sc = [scores[e:e + 1, :] for e in range(e_tot)]
    gsum = []
    for g in range(ng):
        a, b, c, d = sel[epg * g:epg * g + epg]
        hi1, lo1, hi2, lo2 = jnp.maximum(a, b), jnp.minimum(a, b), jnp.maximum(c, d), jnp.minimum(c, d)
        gsum.append(jnp.maximum(hi1, hi2) + jnp.maximum(jnp.minimum(hi1, hi2), jnp.maximum(lo1, lo2)))
    best = gsum[0]
    gi = jnp.zeros_like(best, dtype=jnp.int32)
    for g in range(1, ng):
        upd = gsum[g] > best
        gi = jnp.where(upd, g, gi)
        best = jnp.where(upd, gsum[g], best)
    picked = []
    for e in range(e_tot):
        g, j = divmod(e, epg)
        ahead = jnp.zeros_like(gi)
        for j2 in range(epg):
            if j2 == j:
                continue
            o = sel[epg * g + j2]
            before = (o >= sel[e]) if j2 < j else (o > sel[e])
            ahead = ahead + before.astype(jnp.int32)
        picked.append((gi == g) & (ahead < 2))
    wk = [jnp.where(picked[e], sc[e], 0.0) for e in range(e_tot)]
    tot = wk[0]
    for e in range(1, e_tot):
        tot = tot + wk[e]
    return jnp.concatenate([w / tot for w in wk], axis=0)


def _router_kernel(x_ref, nw_ref, sh_ref, sc_ref, wr_ref, br_ref, h_ref, g_ref):
    h = _rms_mod(x_ref[0], nw_ref[...], sh_ref[0], sc_ref[0])
    h_ref[0] = h.astype(h_ref.dtype)
    logits_t = lax.dot_general(wr_ref[...], h, NT_DIMS, preferred_element_type=F32, precision=HIGHEST)
    g_ref[0] = _route(logits_t, br_ref[...])


def router(x, nw, shift, scale, w_router, b_router, tm, h_dtype):
    b, l, d = x.shape
    e = w_router.shape[1]
    row = lambda i, j: (i, j, 0)
    const = lambda i, j: (0, 0)
    return pl.pallas_call(
        _router_kernel,
        grid=(b, l // tm),
        in_specs=[pl.BlockSpec((1, tm, d), row), pl.BlockSpec((1, d), const),
                  _mod_spec(shift, tm), _mod_spec(scale, tm),
                  pl.BlockSpec((e, d), const), pl.BlockSpec((e, 1), const)],
        out_specs=[pl.BlockSpec((1, tm, d), row), pl.BlockSpec((1, e, tm), lambda i, j: (i, 0, j))],
        out_shape=[jax.ShapeDtypeStruct((b, l, d), h_dtype), jax.ShapeDtypeStruct((b, e, l), F32)],
        name="router",
    )(x, nw.reshape(1, d), shift, scale, w_router.T, b_router.reshape(e, 1))


def _moe_dense_kernel(h_ref, gt_ref, wg_ref, wu_ref, wd_ref, x_ref, g6_ref, o_ref, acc_ref):
    e = pl.program_id(2)

    @pl.when(e == 0)
    def _():
        acc_ref[...] = jnp.zeros_like(acc_ref)

    h = h_ref[0]
    hg = _mm(h, wg_ref[0])
    hu = _mm(h, wu_ref[0])
    act = _silu(hg) * hu * gt_ref[0, 0]
    acc_ref[...] += _mm(act, wd_ref[0])

    @pl.when(e == pl.num_programs(2) - 1)
    def _():
        o_ref[0] = x_ref[0] + g6_ref[0] * acc_ref[...]


def moe_dense(h, gates_col, w_gate, w_up, w_down, x, gate6, tm):
    b, l, d = x.shape
    e, _, f = w_gate.shape
    row = lambda i, j, k: (i, j, 0)
    return pl.pallas_call(
        _moe_dense_kernel,
        grid=(b, l // tm, e),
        in_specs=[pl.BlockSpec((1, tm, d), row),
                  pl.BlockSpec((1, 1, tm, 1), lambda i, j, k: (i, k, j, 0)),
                  pl.BlockSpec((1, d, f), lambda i, j, k: (k, 0, 0)),
                  pl.BlockSpec((1, d, f), lambda i, j, k: (k, 0, 0)),
                  pl.BlockSpec((1, f, d), lambda i, j, k: (k, 0, 0)),
                  pl.BlockSpec((1, tm, d), row),
                  _mod_spec(gate6, tm)],
        out_specs=pl.BlockSpec((1, tm, d), row),
        out_shape=jax.ShapeDtypeStruct((b, l, d), F32),
        scratch_shapes=[pltpu.VMEM((tm, d), F32)],
        name="moe_dense",
    )(h, gates_col, w_gate, w_up, w_down, x, gate6)


def moe_block(x, nw, shift, scale, gate6, w_router, b_router, w_gate, w_up, w_down, tm_route, tm_moe, h_dtype):
    b, l, d = x.shape
    h, gates_t = router(x, nw, shift, scale, w_router, b_router, tm_route, h_dtype)
    return moe_dense(h, gates_t.reshape(b, -1, l, 1), w_gate, w_up, w_down, x, gate6, tm_moe)


def _final_kernel(x_ref, nw_ref, o_ref):
    x = x_ref[0]
    o_ref[0] = x * lax.rsqrt(jnp.mean(x * x, axis=-1, keepdims=True) + EPS) * nw_ref[...]


def final_norm(x, nw, tm):
    b, l, d = x.shape
    row = lambda i, j: (i, j, 0)
    return pl.pallas_call(
        _final_kernel,
        grid=(b, l // tm),
        in_specs=[pl.BlockSpec((1, tm, d), row), pl.BlockSpec((1, d), lambda i, j: (0, 0))],
        out_specs=pl.BlockSpec((1, tm, d), row),
        out_shape=jax.ShapeDtypeStruct((b, l, d), F32),
        name="final_norm",
    )(x, nw.reshape(1, d))


PAGES_PER_STEP = 8


def _sb_attn_sample_kernel(pt_ref, bias_ref, q_ref, *rest):
    npp = PAGES_PER_STEP
    k_refs, v_refs = rest[:npp], rest[npp:2 * npp]
    o_ref, acc_ref, car_ref = rest[2 * npp:]
    s = pl.program_id(1)
    nh = SB_HEADS

    @pl.when(s == 0)
    def _():
        acc_ref[...] = jnp.zeros_like(acc_ref)
        car_ref[...] = jnp.zeros_like(car_ref)

    pg = k_refs[0].shape[-1]
    qb = jnp.broadcast_to(q_ref[0], (nh, SB_HEAD_DIM, pg))
    bias_col = bias_ref[...]
    lane = lax.broadcasted_iota(jnp.int32, (nh, pg), 1)

    for r in range(npp):
        z = jnp.sum(k_refs[r][...] * qb, axis=1) + bias_col
        sp = _softplus(z)
        l1m = -sp
        c = l1m
        d = 1
        while d < pg:
            c = c + jnp.where(lane + d < pg, pltpu.roll(c, pg - d, 1), 0.0)
            d *= 2
        w = jnp.exp(z - sp + car_ref[...] + c - l1m)
        car_ref[...] += jnp.broadcast_to(c[:, 0:1], (nh, pg))
        acc_ref[...] += v_refs[r][...] * w[:, None, :]

    @pl.when(s == pl.num_programs(1) - 1)
    def _():
        o_ref[0] = jnp.sum(acc_ref[...], axis=-1)


def sb_attn_sample(q, cache_kt, cache_vt, layer, page_table, bias):
    n, nh, dh = q.shape
    n_pages = page_table.shape[1]
    pg = cache_kt.shape[-1]
    npp = PAGES_PER_STEP
    nsteps = n_pages // npp

    def page_spec(r):
        def imap(b, s, pt):
            return (layer, pt[b * n_pages + (n_pages - 1 - (s * npp + r))], 0, 0, 0)
        return pl.BlockSpec((None, None, nh, dh, pg), imap)

    grid_spec = pltpu.PrefetchScalarGridSpec(
        num_scalar_prefetch=1,
        grid=(n, nsteps),
        in_specs=([pl.BlockSpec((nh, 1), lambda b, s, pt: (0, 0)),
                   pl.BlockSpec((1, nh, dh, 1), lambda b, s, pt: (b, 0, 0, 0))]
                  + [page_spec(r) for r in range(npp)] * 2),
        out_specs=pl.BlockSpec((1, nh, dh), lambda b, s, pt: (b, 0, 0)),
        scratch_shapes=[pltpu.VMEM((nh, dh, pg), F32), pltpu.VMEM((nh, pg), F32)],
    )
    return pl.pallas_call(
        _sb_attn_sample_kernel,
        grid_spec=grid_spec,
        out_shape=jax.ShapeDtypeStruct((n, nh, dh), F32),
        name="sb_attn_sample",
    )(page_table.reshape(-1), bias.reshape(nh, 1), q.reshape(n, nh, dh, 1),
      *([cache_kt] * npp), *([cache_vt] * npp))


def _conv_sample_kernel(u_ref, st_ref, w_ref, b_ref, lnw_ref, lnb_ref, o_ref):
    km1 = CONV_K - 1
    w = w_ref[...]
    y = jnp.sum(st_ref[...] * w[None, :km1, :], axis=1) + u_ref[...] * w[km1:, :] + b_ref[...]
    o_ref[...] = _layernorm_silu(y, lnw_ref[...], lnb_ref[...])


def conv_sample(u, state, conv_w, conv_b, ln_w, ln_b):
    n, c = u.shape
    return pl.pallas_call(
        _conv_sample_kernel,
        out_shape=jax.ShapeDtypeStruct((n, c), F32),
        name="conv_sample",
    )(u, state, conv_w, conv_b.reshape(1, c), ln_w.reshape(1, c), ln_b.reshape(1, c))


def _s5_sample_kernel(u_ref, h0r_ref, h0i_ref, ar_ref, ai_ref, bbr_ref, bbi_ref, cr_ref, ci_ref, d_ref,
                      y_ref, hr_ref, hi_ref):
    u = u_ref[...]
    ein = functools.partial(jnp.einsum, precision=HIGHEST, preferred_element_type=F32)
    hr0, hi0 = h0r_ref[...], h0i_ref[...]
    ar, ai = ar_ref[...], ai_ref[...]
    hr = ar * hr0 - ai * hi0 + ein('gnc,gcp->gnp', u, bbr_ref[...])
    hi = ar * hi0 + ai * hr0 + ein('gnc,gcp->gnp', u, bbi_ref[...])
    hr_ref[...] = hr
    hi_ref[...] = hi
    y = ein('gnp,gpc->gnc', hr, cr_ref[...]) - ein('gnp,gpc->gnc', hi, ci_ref[...]) + d_ref[...] * u
    y_ref[...] = jax.nn.gelu(y)


def s5_sample(u, h0_re, h0_im, ar, ai, bbr, bbi, c_re, c_im, d):
    n, dm = u.shape
    g, p, c = bbr.shape
    t = lambda a: a.transpose(1, 0, 2)
    sh = jax.ShapeDtypeStruct
    y, hr, hi = pl.pallas_call(
        _s5_sample_kernel,
        out_shape=[sh((g, n, c), F32), sh((g, n, p), F32), sh((g, n, p), F32)],
        name="s5_sample",
    )(t(u.reshape(n, g, c)), t(h0_re), t(h0_im), ar[:, None, :], ai[:, None, :],
      bbr.transpose(0, 2, 1), bbi.transpose(0, 2, 1), c_re.transpose(0, 2, 1), c_im.transpose(0, 2, 1),
      d.reshape(g, 1, c))
    return t(y).reshape(n, dm), t(hr), t(hi)


def kernel(x_prompt, x_sample, c_prompt, c_sample, cache_k, cache_v, page_table, state_conv, state_ssm_re, state_ssm_im, w_ada, b_ada, norm_w, final_norm_w, w_in_mix, w_out_mix, sb_bias, conv_w, conv_b, conv_ln_w, conv_ln_b, ssm_lambda_re, ssm_lambda_im, ssm_b_re, ssm_b_im, ssm_c_re, ssm_c_im, ssm_d, ssm_log_dt, w_glu, w_router, b_router, w_gate, w_up, w_down):
    nb, l, d = x_prompt.shape
    ns = x_sample.shape[0]
    depth = w_ada.shape[0]
    n_attn = cache_k.shape[0]
    assert nb == SUBLANES and x_sample.shape[1] == 1
    tm = 512

    ada = ada_all(jnp.concatenate([c_prompt, c_sample], axis=0), w_ada, b_ada)
    ada = ada.reshape(depth, nb + ns, N_ADA, d)
    cache_kt = cache_k.transpose(0, 1, 3, 4, 2)
    cache_vt = cache_v.transpose(0, 1, 3, 4, 2)

    xp = x_prompt
    xs = x_sample.reshape(1, ns, d)
    kp_l, vp_l, ks_l, vs_l, cvp_l, cvs_l = [], [], [], [], [], []
    srp_l, sip_l, srs_l, sis_l = [], [], [], []
    for layer in range(depth):
        ap = [ada[layer, :nb, j].reshape(nb, 1, d) for j in range(N_ADA)]
        asm = [ada[layer, nb:, j].reshape(1, ns, d) for j in range(N_ADA)]
        nw = norm_w[layer]
        i = layer // 2
        if layer % 2 == 0:
            w_in_b, w_out_b = w_in_mix[i].astype(BF16), w_out_mix[i].astype(BF16)
            q, k, v, u = inproj(xp, nw[0], ap[0], ap[1], w_in_b, tm, BF16)
            oa = sb_attn_prompt(q, k, v, sb_bias[i])
            oc = conv_prompt(u, conv_w[i], conv_b[i], conv_ln_w[i], conv_ln_b[i])
            xp = outproj(xp, oa, oc, w_out_b, ap[2], tm)
            kp_l.append(k.reshape(nb, l, SB_HEADS, SB_HEAD_DIM))
            vp_l.append(v.reshape(nb, l, SB_HEADS, SB_HEAD_DIM))
            cvp_l.append(u[:, l - (CONV_K - 1):, :])

            qs, ks_, vs_, us = inproj(xs, nw[0], asm[0], asm[1], w_in_mix[i], ns, F32)
            oas = sb_attn_sample(qs.reshape(ns, SB_HEADS, SB_HEAD_DIM), cache_kt, cache_vt, i, page_table, sb_bias[i])
            ocs = conv_sample(us[0], state_conv[i], conv_w[i], conv_b[i], conv_ln_w[i], conv_ln_b[i])
            xs = outproj(xs, oas.reshape(1, ns, SB_WIDTH), ocs[None], w_out_mix[i], asm[2], ns)
            ks_l.append(ks_.reshape(ns, 1, SB_HEADS, SB_HEAD_DIM))
            vs_l.append(vs_.reshape(ns, 1, SB_HEADS, SB_HEAD_DIM))
            cvs_l.append(jnp.concatenate([state_conv[i][:, 1:, :], us.reshape(ns, 1, CONV_WIDTH)], axis=1))
        else:
            ar, ai, bbr, bbi = s5_discretize(ssm_lambda_re[i], ssm_lambda_im[i], ssm_b_re[i], ssm_b_im[i],
                                             ssm_log_dt[i])
            kb, cb, a_chunk = s5_chunk_operators(ar, ai, bbr, bbi, ssm_c_re[i], ssm_c_im[i], ssm_d[i])
            hp = normmod(xp, nw[0], ap[0], ap[1], tm, F32)
            yp, srp, sip = s5_prompt(hp, kb, cb, a_chunk)
            xp = glu_residual(xp, yp, w_glu[i].astype(BF16), ap[2], tm)
            srp_l.append(srp)
            sip_l.append(sip)

            hs = normmod(xs, nw[0], asm[0], asm[1], ns, F32)
            ys, srs, sis = s5_sample(hs[0], state_ssm_re[i], state_ssm_im[i], ar, ai, bbr, bbi,
                                     ssm_c_re[i], ssm_c_im[i], ssm_d[i])
            xs = glu_residual(xs, ys[None], w_glu[i], asm[2], ns)
            srs_l.append(srs)
            sis_l.append(sis)
        xp = moe_block(xp, nw[1], ap[3], ap[4], ap[5], w_router, b_router,
                       w_gate[layer].astype(BF16), w_up[layer].astype(BF16), w_down[layer].astype(BF16),
                       min(l, 1024), min(l, 1024), BF16)
        xs = moe_block(xs, nw[1], asm[3], asm[4], asm[5], w_router, b_router,
                       w_gate[layer], w_up[layer], w_down[layer], ns, ns, F32)
    y_prompt = final_norm(xp, final_norm_w, tm)
    y_sample = final_norm(xs, final_norm_w, ns).reshape(ns, 1, d)
    return (y_prompt, y_sample,
            jnp.stack(kp_l), jnp.stack(vp_l), jnp.stack(ks_l), jnp.stack(vs_l),
            jnp.stack(cvp_l), jnp.stack(cvs_l),
            jnp.stack(srp_l), jnp.stack(sip_l), jnp.stack(srs_l), jnp.stack(sis_l))
```

```python
import functools
import math

import jax
import jax.numpy as jnp
from jax import lax
from jax.experimental import pallas as pl
from jax.experimental.pallas import tpu as pltpu

F32 = jnp.float32
BF16 = jnp.bfloat16
HIGHEST = lax.Precision.HIGHEST

EPS = 1e-6
LN_EPS = 1e-5
SB_HEADS = 8
SB_HEAD_DIM = 64
SB_WIDTH = SB_HEADS * SB_HEAD_DIM
CONV_WIDTH = 512
CONV_K = 31
SSM_GROUP = 16
SSM_STATE = 64
N_EXPERTS = 16
EXPERTS_PER_GROUP = 4
N_ADA = 6

LANES = 128
SUBLANES = 8
S5_CHUNK = SUBLANES
NT_DIMS = (((1,), (1,)), ((), ()))


def _dot(a, b):
    return jnp.dot(a, b, preferred_element_type=F32)


def _dotf(a, b):
    return jnp.dot(a, b, preferred_element_type=F32, precision=HIGHEST)


def _mm(a, w):
    if w.dtype == BF16:
        return _dot(a.astype(BF16), w)
    return _dotf(a, w)


def _silu(x):
    return x * jax.nn.sigmoid(x)


def _rms_mod(x, nw, shift, scale):
    y = x * lax.rsqrt(jnp.mean(x * x, axis=-1, keepdims=True) + EPS) * nw
    return y * (1 + scale) + shift


def _mod_spec(a, tm):
    d = a.shape[-1]
    if a.shape[1] == 1:
        return pl.BlockSpec((1, 1, d), lambda i, j, *_: (i, 0, 0))
    return pl.BlockSpec((1, tm, d), lambda i, j, *_: (i, j, 0))


def _weight_spec(w):
    if isinstance(w, tuple):
        stack, layer = w
        return pl.BlockSpec((None,) + stack.shape[1:], lambda i, j: (layer, 0, 0))
    return pl.BlockSpec(w.shape, lambda i, j: (0, 0))


def _weight_arr(w):
    return w[0] if isinstance(w, tuple) else w


def _split_bf16(x):
    hi = x.astype(BF16)
    lo = (x - hi.astype(F32)).astype(BF16)
    return hi, lo


def _ada_kernel(c_ref, w_ref, b_ref, o_ref):
    o_ref[0] = _dotf(_silu(c_ref[...]), w_ref[0]) + b_ref[0]


def ada_all(c, w_ada, b_ada):
    depth, d, n = w_ada.shape
    r = c.shape[0]
    tn = 1024
    return pl.pallas_call(
        _ada_kernel,
        grid=(depth, n // tn),
        in_specs=[pl.BlockSpec((r, d), lambda l, j: (0, 0)),
                  pl.BlockSpec((1, d, tn), lambda l, j: (l, 0, j)),
                  pl.BlockSpec((1, 1, tn), lambda l, j: (l, 0, j))],
        out_specs=pl.BlockSpec((1, r, tn), lambda l, j: (l, 0, j)),
        out_shape=jax.ShapeDtypeStruct((depth, r, n), F32),
        name="ada",
    )(c, w_ada, b_ada.reshape(depth, 1, n))


def _inproj_kernel(x_ref, nw_ref, sh_ref, sc_ref, w_ref, q_ref, k_ref, v_ref, u_ref):
    h = _rms_mod(x_ref[0], nw_ref[...], sh_ref[0], sc_ref[0])
    p = _mm(h, w_ref[...])
    w = SB_WIDTH
    q_ref[0] = (p[:, :w] * (1.0 / math.sqrt(SB_HEAD_DIM))).astype(q_ref.dtype)
    k_ref[0] = p[:, w:2 * w]
    v_ref[0] = p[:, 2 * w:3 * w]
    a = p[:, 3 * w:3 * w + CONV_WIDTH]
    g = p[:, 3 * w + CONV_WIDTH:]
    u_ref[0] = a * jax.nn.sigmoid(g)


def inproj(x, nw, shift, scale, w_in, tm, q_dtype):
    b, l, d = x.shape
    row = lambda i, j: (i, j, 0)
    o = jax.ShapeDtypeStruct((b, l, SB_WIDTH), F32)
    return pl.pallas_call(
        _inproj_kernel,
        grid=(b, l // tm),
        in_specs=[pl.BlockSpec((1, tm, d), row),
                  pl.BlockSpec((1, d), lambda i, j: (0, 0)),
                  _mod_spec(shift, tm), _mod_spec(scale, tm),
                  _weight_spec(w_in)],
        out_specs=[pl.BlockSpec((1, tm, SB_WIDTH), row)] * 4,
        out_shape=[jax.ShapeDtypeStruct((b, l, SB_WIDTH), q_dtype), o, o, o],
        name="inproj",
    )(x, nw.reshape(1, d), shift, scale, _weight_arr(w_in))


ATT_T = 512
ATT_SUB = LANES


def _softplus(z):
    return jnp.maximum(z, 0.0) + jnp.log(1.0 + jnp.exp(-jnp.abs(z)))


def _sb_attn_kernel(bias_ref, q_ref, k_ref, v_ref, o_ref, acc_ref, run_ref):
    t, sub = ATT_T, ATT_SUB
    hp = pl.program_id(1)
    i = pl.program_id(2)
    q2 = q_ref[0]
    lane = lax.broadcasted_iota(jnp.int32, (t, LANES), 1)
    head_of_lane = lane // SB_HEAD_DIM
    qm = [jnp.where(head_of_lane == hh, q2, jnp.zeros_like(q2)) for hh in range(2)]
    bias = [bias_ref[2 * hp + hh] for hh in range(2)]
    r_io = lax.broadcasted_iota(jnp.int32, (sub, 2 * sub), 0)
    c_io = lax.broadcasted_iota(jnp.int32, (sub, 2 * sub), 1)
    tri_ones = ((r_io > c_io) | (c_io >= sub)).astype(BF16)
    acc_ref[...] = jnp.zeros_like(acc_ref)
    run_ref[...] = jnp.zeros_like(run_ref)
    causal = lax.broadcasted_iota(jnp.int32, (t, t), 1) < lax.broadcasted_iota(jnp.int32, (t, t), 0)

    def chunk(j, masked):
        start = pl.multiple_of(j * t, t)
        kb = k_ref[0, pl.ds(start, t), :].astype(BF16)
        vb = v_ref[0, pl.ds(start, t), :].astype(BF16)
        for hh in range(2):
            z = lax.dot_general(qm[hh], kb, NT_DIMS, preferred_element_type=F32) + bias[hh]
            sp = _softplus(z)
            if masked:
                sp = jnp.where(causal, sp, 0.0)
            spb = sp.astype(BF16)
            run = run_ref[hh]
            ws = []
            for blk in range(t // sub - 1, -1, -1):
                cols = slice(blk * sub, (blk + 1) * sub)
                sums = _dot(spb[:, cols], tri_ones)
                w = jnp.exp(z[:, cols] - sp[:, cols] - (run + sums[:, :sub]))
                if masked:
                    w = jnp.where(causal[:, cols], w, 0.0)
                ws.append(w.astype(BF16))
                run = run + sums[:, sub:]
            run_ref[hh] = run
            acc_ref[hh] += _dot(jnp.concatenate(ws[::-1], axis=1), vb)

    chunk(i, True)

    def body(n, c):
        chunk(i - 1 - n, False)
        return c

    lax.fori_loop(0, i, body, 0)
    o_ref[0] = jnp.where(head_of_lane == 0, acc_ref[0], acc_ref[1]).astype(o_ref.dtype)


def sb_attn_prompt(q, k, v, bias):
    b, l, w = q.shape
    t = ATT_T
    return pl.pallas_call(
        _sb_attn_kernel,
        grid=(b, w // LANES, l // t),
        in_specs=[pl.BlockSpec(memory_space=pltpu.SMEM),
                  pl.BlockSpec((1, t, LANES), lambda bi, hp, i: (bi, i, hp)),
                  pl.BlockSpec((1, l, LANES), lambda bi, hp, i: (bi, 0, hp)),
                  pl.BlockSpec((1, l, LANES), lambda bi, hp, i: (bi, 0, hp))],
        out_specs=pl.BlockSpec((1, t, LANES), lambda bi, hp, i: (bi, i, hp)),
        out_shape=jax.ShapeDtypeStruct((b, l, w), BF16),
        scratch_shapes=[pltpu.VMEM((2, t, LANES), F32), pltpu.VMEM((2, t, ATT_SUB), F32)],
        name="sb_attn_prompt",
    )(bias, q, k, v)


CONV_TR = 32
CONV_PAD = 32


def _layernorm_silu(y, lnw, lnb):
    mu = jnp.mean(y, axis=-1, keepdims=True)
    yc = y - mu
    var = jnp.mean(yc * yc, axis=-1, keepdims=True)
    return _silu(yc * lax.rsqrt(var + LN_EPS) * lnw + lnb)


def _conv_kernel(u_ref, w_ref, b_ref, lnw_ref, lnb_ref, o_ref, full_ref):
    l = u_ref.shape[1]
    tr = CONV_TR
    full_ref[0:CONV_PAD, :] = jnp.zeros((CONV_PAD, CONV_WIDTH), F32)
    full_ref[CONV_PAD:CONV_PAD + l, :] = u_ref[0]
    full_ref[CONV_PAD + l:, :] = jnp.zeros((SUBLANES, CONV_WIDTH), F32)
    first = CONV_PAD - (CONV_K - 1)
    bias = b_ref[...]
    lnw = lnw_ref[...]
    lnb = lnb_ref[...]

    def tile(t, c):
        r0 = pl.multiple_of(t * tr, tr)
        y = jnp.zeros((tr, CONV_WIDTH), F32)
        for r in range(SUBLANES):
            g = None
            for m in range(first, first + CONV_K):
                if m % SUBLANES != r:
                    continue
                win = full_ref[pl.ds(r0 + (m - r), tr + SUBLANES), :]
                term = w_ref[m - first:m - first + 1, :] * win
                g = term if g is None else g + term
            if g is not None:
                y = y + g[r:r + tr, :]
        o_ref[0, pl.ds(r0, tr), :] = _layernorm_silu(y + bias, lnw, lnb).astype(o_ref.dtype)
        return c

    lax.fori_loop(0, l // tr, tile, 0)


def conv_prompt(u, conv_w, conv_b, ln_w, ln_b):
    b, l, c = u.shape
    vec = lambda i: (0, 0)
    return pl.pallas_call(
        _conv_kernel,
        grid=(b,),
        in_specs=[pl.BlockSpec((1, l, c), lambda i: (i, 0, 0)),
                  pl.BlockSpec((CONV_K, c), vec),
                  pl.BlockSpec((1, c), vec), pl.BlockSpec((1, c), vec), pl.BlockSpec((1, c), vec)],
        out_specs=pl.BlockSpec((1, l, c), lambda i: (i, 0, 0)),
        out_shape=jax.ShapeDtypeStruct((b, l, c), BF16),
        scratch_shapes=[pltpu.VMEM((CONV_PAD + l + SUBLANES, c), F32)],
        name="conv_prompt",
    )(u, conv_w, conv_b.reshape(1, c), ln_w.reshape(1, c), ln_b.reshape(1, c))


def _outproj_kernel(x_ref, oa_ref, oc_ref, w_ref, g_ref, o_ref):
    m = _mm(oa_ref[0], w_ref[:SB_WIDTH, :]) + _mm(oc_ref[0], w_ref[SB_WIDTH:, :])
    o_ref[0] = x_ref[0] + g_ref[0] * m


def outproj(x, oa, oc, w_out, gate, tm):
    b, l, d = x.shape
    row = lambda i, j: (i, j, 0)
    return pl.pallas_call(
        _outproj_kernel,
        grid=(b, l // tm),
        in_specs=[pl.BlockSpec((1, tm, d), row),
                  pl.BlockSpec((1, tm, SB_WIDTH), row),
                  pl.BlockSpec((1, tm, CONV_WIDTH), row),
                  _weight_spec(w_out),
                  _mod_spec(gate, tm)],
        out_specs=pl.BlockSpec((1, tm, d), row),
        out_shape=jax.ShapeDtypeStruct((b, l, d), F32),
        name="outproj",
    )(x, oa, oc, _weight_arr(w_out), gate)


def _normmod_kernel(x_ref, nw_ref, sh_ref, sc_ref, o_ref):
    o_ref[0] = _rms_mod(x_ref[0], nw_ref[...], sh_ref[0], sc_ref[0]).astype(o_ref.dtype)


def normmod(x, nw, shift, scale, tm, dtype):
    b, l, d = x.shape
    row = lambda i, j: (i, j, 0)
    return pl.pallas_call(
        _normmod_kernel,
        grid=(b, l // tm),
        in_specs=[pl.BlockSpec((1, tm, d), row), pl.BlockSpec((1, d), lambda i, j: (0, 0)),
                  _mod_spec(shift, tm), _mod_spec(scale, tm)],
        out_specs=pl.BlockSpec((1, tm, d), row),
        out_shape=jax.ShapeDtypeStruct((b, l, d), dtype),
        name="normmod",
    )(x, nw.reshape(1, d), shift, scale)


def _cmul(ar, ai, br, bi):
    return ar * br - ai * bi, ar * bi + ai * br


def s5_discretize(lam_re, lam_im, b_re, b_im, log_dt):
    dt = jnp.exp(log_dt)[:, None]
    mag = jnp.exp(lam_re * dt)
    ar, ai = mag * jnp.cos(lam_im * dt), mag * jnp.sin(lam_im * dt)
    den = lam_re * lam_re + lam_im * lam_im
    fr = ((ar - 1) * lam_re + ai * lam_im) / den
    fi = (ai * lam_re - (ar - 1) * lam_im) / den
    bbr, bbi = _cmul(fr[..., None], fi[..., None], b_re, b_im)
    return ar, ai, bbr, bbi


def s5_chunk_operators(ar, ai, bbr, bbi, c_re, c_im, d):
    g, p, c = bbr.shape
    ch = S5_CHUNK
    gl = LANES // c
    nt = g // gl
    pr, pi = [jnp.ones_like(ar)], [jnp.zeros_like(ai)]
    for _ in range(ch):
        nr, ni = _cmul(pr[-1], pi[-1], ar, ai)
        pr.append(nr)
        pi.append(ni)
    pr, pi = jnp.stack(pr), jnp.stack(pi)
    car, cai = _cmul(c_re[None], c_im[None], pr[:, :, None, :], pi[:, :, None, :])
    ein = functools.partial(jnp.einsum, precision=HIGHEST)
    ktau = ein('tgcp,gpd->tgcd', car[:ch], bbr) - ein('tgcp,gpd->tgcd', cai[:ch], bbi)
    ktau = ktau.at[0].add(jnp.eye(c, dtype=F32)[None] * d.reshape(g, c)[:, :, None])
    eye = jnp.eye(gl, dtype=F32)
    kblk = (ktau.reshape(ch, nt, gl, c, c).transpose(0, 1, 2, 4, 3)[:, :, :, :, None, :]
            * eye[None, None, :, None, :, None]).reshape(ch, nt, LANES, LANES)
    s_idx = jnp.arange(ch)
    tau = s_idx[None, :] - s_idx[:, None]
    kbig = jnp.where((tau >= 0)[:, :, None, None, None], kblk[jnp.clip(tau, 0, ch - 1)], 0.0)
    kbig = kbig.transpose(2, 0, 3, 1, 4).reshape(nt, ch * LANES, ch * LANES)

    def state_cols(w):
        w = w.reshape(ch, nt, gl, p, c).transpose(0, 1, 2, 4, 3)[:, :, :, :, None, :]
        w = w * eye[None, None, :, None, :, None]
        return w.reshape(ch, nt, LANES, gl * p).transpose(1, 0, 2, 3).reshape(nt, ch * LANES, gl * p)

    rev = ch - 1 - s_idx
    wr, wi = _cmul(pr[rev][..., None], pi[rev][..., None], bbr[None], bbi[None])
    kb = jnp.concatenate([kbig, state_cols(wr), state_cols(wi)], axis=-1)

    def state_rows(w):
        w = w.reshape(ch, nt, gl, c, p).transpose(1, 4, 0, 2, 3)[:, None]
        w = w * eye[None, :, None, None, :, None]
        return w.reshape(nt, gl * p, ch * LANES)

    cb = jnp.concatenate([state_rows(car[1:]), -state_rows(cai[1:])], axis=1)
    pr8, pi8 = pr[ch].reshape(nt, gl * p), pi[ch].reshape(nt, gl * p)
    a_chunk = jnp.stack([jnp.concatenate([pr8, pr8], -1), jnp.concatenate([-pi8, pi8], -1)], axis=1)
    return kb.astype(BF16), cb.astype(BF16), a_chunk


def _s5_kernel(u_ref, kb_ref, cb_ref, a_ref, y_ref, hl_ref, s_scr, y_scr):
    ch = S5_CHUNK
    nk = u_ref.shape[1] // ch
    width = ch * LANES
    ucat = jnp.concatenate([u_ref[0, pl.ds(s, nk, stride=ch), :].astype(BF16) for s in range(ch)], axis=1)
    kb = kb_ref[0]
    s_scr[...] = _dot(ucat, kb[:, width:])
    a_same = a_ref[0, 0:1, :]
    a_swap = a_ref[0, 1:2, :]
    half = a_same.shape[-1] // 2

    def rows(gi, h):
        base = pl.multiple_of(gi * SUBLANES, SUBLANES)
        blk = s_scr[pl.ds(base, SUBLANES), :]
        before = []
        for j in range(SUBLANES):
            before.append(h)
            h = h * a_same + pltpu.roll(h, half, 1) * a_swap + blk[j:j + 1, :]
        s_scr[pl.ds(base, SUBLANES), :] = jnp.concatenate(before, axis=0)
        return h

    hl_ref[0, 0] = lax.fori_loop(0, nk // SUBLANES, rows, jnp.zeros((1, 2 * half), F32))
    y = _dot(ucat, kb[:, :width]) + _dot(s_scr[...].astype(BF16), cb_ref[0])
    for t in range(ch):
        y_scr[pl.ds(t, nk, stride=ch), :] = jax.nn.gelu(y[:, t * LANES:(t + 1) * LANES])
    y_ref[0] = y_scr[...].astype(y_ref.dtype)


def s5_prompt(u, kb, cb, a_chunk):
    b, l, dm = u.shape
    nt = kb.shape[0]
    ch = S5_CHUNK
    sw = cb.shape[1]
    y, hl = pl.pallas_call(
        _s5_kernel,
        grid=(nt, b),
        in_specs=[pl.BlockSpec((1, l, LANES), lambda j, i: (i, 0, j)),
                  pl.BlockSpec((1,) + kb.shape[1:], lambda j, i: (j, 0, 0)),
                  pl.BlockSpec((1,) + cb.shape[1:], lambda j, i: (j, 0, 0)),
                  pl.BlockSpec((1, 2, sw), lambda j, i: (j, 0, 0))],
        out_specs=[pl.BlockSpec((1, l, LANES), lambda j, i: (i, 0, j)),
                   pl.BlockSpec((1, 1, 1, sw), lambda j, i: (i, j, 0, 0))],
        out_shape=[jax.ShapeDtypeStruct((b, l, dm), BF16), jax.ShapeDtypeStruct((b, nt, 1, sw), F32)],
        scratch_shapes=[pltpu.VMEM((l // ch, sw), F32), pltpu.VMEM((l, LANES), F32)],
        name="s5_prompt",
    )(u, kb, cb, a_chunk)
    hl = hl.reshape(b, nt, 2, -1, SSM_STATE)
    return y, hl[:, :, 0].reshape(b, -1, SSM_STATE), hl[:, :, 1].reshape(b, -1, SSM_STATE)


def _glu_kernel(x_ref, y_ref, w_ref, g_ref, o_ref):
    z = _mm(y_ref[0], w_ref[...])
    d = o_ref.shape[-1]
    o_ref[0] = x_ref[0] + g_ref[0] * (z[:, :d] * jax.nn.sigmoid(z[:, d:]))


def glu_residual(x, y, w_glu, gate, tm):
    b, l, d = x.shape
    row = lambda i, j: (i, j, 0)
    return pl.pallas_call(
        _glu_kernel,
        grid=(b, l // tm),
        in_specs=[pl.BlockSpec((1, tm, d), row), pl.BlockSpec((1, tm, d), row),
                  _weight_spec(w_glu),
                  _mod_spec(gate, tm)],
        out_specs=pl.BlockSpec((1, tm, d), row),
        out_shape=jax.ShapeDtypeStruct((b, l, d), F32),
        name="glu_residual",
    )(x, y, _weight_arr(w_glu), gate)


def _route(logits_t, b_col):
    e_tot = logits_t.shape[0]
    epg = EXPERTS_PER_GROUP
    ng = e_tot // epg
    scores = jax.nn.sigmoid(logits_t)
    sel_all = scores + b_col
    sel = [sel_all[e:e + 1, :] for e in range(e_tot)]
    sc = [scores[e:e + 1, :] for e in range(e_tot)]
    gsum = []
    for g in range(ng):
        a, b, c, d = sel[epg * g:epg * g + epg]
        hi1, lo1, hi2, lo2 = jnp.maximum(a, b), jnp.minimum(a, b), jnp.maximum(c, d), jnp.minimum(c, d)
        gsum.append(jnp.maximum(hi1, hi2) + jnp.maximum(jnp.minimum(hi1, hi2), jnp.maximum(lo1, lo2)))
    best = gsum[0]
    gi = jnp.zeros_like(best, dtype=jnp.int32)
    for g in range(1, ng):
        upd = gsum[g] > best
        gi = jnp.where(upd, g, gi)
        best = jnp.where(upd, gsum[g], best)
    picked = []
    for e in range(e_tot):
        g, j = divmod(e, epg)
        ahead = jnp.zeros_like(gi)
        for j2 in range(epg):
            if j2 == j:
                continue
            o = sel[epg * g + j2]
            before = (o >= sel[e]) if j2 < j else (o > sel[e])
            ahead = ahead + before.astype(jnp.int32)
        picked.append((gi == g) & (ahead < 2))
    wk = [jnp.where(picked[e], sc[e], 0.0) for e in range(e_tot)]
    tot = wk[0]
    for e in range(1, e_tot):
        tot = tot + wk[e]
    return jnp.concatenate([w / tot for w in wk], axis=0)


MOE_SUB = 512


def _router_kernel(x_ref, nw_ref, sh_ref, sc_ref, wr_ref, br_ref, h_ref, g_ref, rank_ref, cnt_ref):
    h = _rms_mod(x_ref[0], nw_ref[...], sh_ref[0], sc_ref[0])
    h_ref[0] = h.astype(h_ref.dtype)
    logits_t = lax.dot_general(wr_ref[...], h, NT_DIMS, preferred_element_type=F32, precision=HIGHEST)
    gates = _route(logits_t, br_ref[...])
    e_tot, l = gates.shape
    sub = min(MOE_SUB, l)
    picked = gates > 0.0
    within = lax.broadcasted_iota(jnp.int32, (e_tot, l), 1) % sub
    c = picked.astype(F32)
    d = 1
    while d < sub:
        c = c + jnp.where(within >= d, pltpu.roll(c, d, 1), 0.0)
        d *= 2
    rank = jnp.where(picked, c - 1.0, -1.0).astype(jnp.int32)
    for e in range(e_tot):
        g_ref[0, e] = gates[e:e + 1, :]
        rank_ref[0, e] = rank[e:e + 1, :]
    cnt_ref[0] = jnp.concatenate([c[:, s * sub + sub - 1:s * sub + sub] for s in range(l // sub)],
                                 axis=1).astype(jnp.int32)


def router(x, nw, shift, scale, w_router, b_router, h_dtype):
    b, l, d = x.shape
    e = w_router.shape[1]
    nsub = l // min(MOE_SUB, l)
    row = lambda i: (i, 0, 0)
    const = lambda i: (0, 0)
    mod = lambda a: pl.BlockSpec((1,) + a.shape[1:], row)
    return pl.pallas_call(
        _router_kernel,
        grid=(b,),
        in_specs=[pl.BlockSpec((1, l, d), row), pl.BlockSpec((1, d), const), mod(shift), mod(scale),
                  pl.BlockSpec((e, d), const), pl.BlockSpec((e, 1), const)],
        out_specs=[pl.BlockSpec((1, l, d), row),
                   pl.BlockSpec((1, e, 1, l), lambda i: (i, 0, 0, 0)),
                   pl.BlockSpec((1, e, 1, l), lambda i: (i, 0, 0, 0)),
                   pl.BlockSpec((1, e, nsub), row)],
        out_shape=[jax.ShapeDtypeStruct((b, l, d), h_dtype), jax.ShapeDtypeStruct((b, e, 1, l), F32),
                   jax.ShapeDtypeStruct((b, e, 1, l), jnp.int32), jax.ShapeDtypeStruct((b, e, nsub), jnp.int32)],
        name="router",
    )(x, nw.reshape(1, d), shift, scale, w_router.T, b_router.reshape(e, 1))


def _moe_dense_kernel(h_ref, gt_ref, wg_ref, wu_ref, wd_ref, x_ref, g6_ref, o_ref, acc_ref):
    e = pl.program_id(2)

    @pl.when(e == 0)
    def _():
        acc_ref[...] = jnp.zeros_like(acc_ref)

    h = h_ref[0]
    hg = _mm(h, wg_ref[0])
    hu = _mm(h, wu_ref[0])
    act = _silu(hg) * hu * gt_ref[0, 0]
    acc_ref[...] += _mm(act, wd_ref[0])

    @pl.when(e == pl.num_programs(2) - 1)
    def _():
        o_ref[0] = x_ref[0] + g6_ref[0] * acc_ref[...]


def moe_dense(h, gates_col, w_gate, w_up, w_down, layer, x, gate6, tm):
    b, l, d = x.shape
    _, e, _, f = w_gate.shape
    row = lambda i, j, k: (i, j, 0)
    wspec = lambda s: pl.BlockSpec((None, 1) + s, lambda i, j, k: (layer, k, 0, 0))
    return pl.pallas_call(
        _moe_dense_kernel,
        grid=(b, l // tm, e),
        in_specs=[pl.BlockSpec((1, tm, d), row),
                  pl.BlockSpec((1, 1, tm, 1), lambda i, j, k: (i, k, j, 0)),
                  wspec((d, f)), wspec((d, f)), wspec((f, d)),
                  pl.BlockSpec((1, tm, d), row),
                  _mod_spec(gate6, tm)],
        out_specs=pl.BlockSpec((1, tm, d), row),
        out_shape=jax.ShapeDtypeStruct((b, l, d), F32),
        scratch_shapes=[pltpu.VMEM((tm, d), F32)],
        name="moe_dense",
    )(h, gates_col, w_gate, w_up, w_down, x, gate6)


MOE_CAP = 128
MOE_TM = 2048
TN_DIMS = (((0,), (0,)), ((), ()))


def _moe_sparse_kernel(cnt_ref, h_ref, rank_ref, gt_ref, wg_ref, wu_ref, wd_ref, x_ref, g6_ref, o_ref):
    bi, j, e = pl.program_id(0), pl.program_id(1), pl.program_id(2)
    tm = o_ref.shape[1]
    sub = min(MOE_SUB, tm)
    nsub = tm // sub
    cap = MOE_CAP

    @pl.when(e == 0)
    def _():
        o_ref[...] = jnp.zeros_like(o_ref)

    row_io = lax.broadcasted_iota(jnp.int32, (cap, sub), 0)

    def chunks(subs, c):
        ps, xs, gs = [], [], []
        for s in subs:
            cols = slice(s * sub, (s + 1) * sub)
            pick = rank_ref[0, 0, :, cols] == row_io + c * cap
            p = pick.astype(BF16)
            ps.append(p)
            xs.append(_dot(p, h_ref[0, cols, :]).astype(BF16))
            gs.append(jnp.sum(jnp.where(pick, gt_ref[0, 0, :, cols], 0.0), axis=1, keepdims=True))
        xg = jnp.concatenate(xs, axis=0)
        act = _silu(_dot(xg, wg_ref[0])) * _dot(xg, wu_ref[0]) * jnp.concatenate(gs, axis=0)
        y = _dot(act.astype(BF16), wd_ref[0]).astype(BF16)
        for n, s in enumerate(subs):
            o_ref[0, s * sub:(s + 1) * sub, :] += lax.dot_general(ps[n], y[n * cap:(n + 1) * cap, :],
                                                                 TN_DIMS, preferred_element_type=F32)

    chunks(list(range(nsub)), 0)
    for s in range(nsub):
        n = cnt_ref[((bi * pl.num_programs(1) + j) * nsub + s) * pl.num_programs(2) + e]

        def more(c, carry, s=s):
            chunks([s], c)
            return carry

        lax.fori_loop(1, (n + cap - 1) // cap, more, 0)

    @pl.when(e == pl.num_programs(2) - 1)
    def _():
        o_ref[0] = x_ref[0] + g6_ref[0] * o_ref[0]


def moe_sparse(h, gates, rank, cnt, w_gate, w_up, w_down, x, gate6):
    b, l, d = x.shape
    e, _, f = w_gate.shape
    tm = min(MOE_TM, l)
    row = lambda i, j, k, c: (i, j, 0)
    per_e = lambda i, j, k, c: (i, k, 0, j)
    wsp = lambda s: pl.BlockSpec((1,) + s, lambda i, j, k, c: (k, 0, 0))
    grid_spec = pltpu.PrefetchScalarGridSpec(
        num_scalar_prefetch=1,
        grid=(b, l // tm, e),
        in_specs=[pl.BlockSpec((1, tm, d), row),
                  pl.BlockSpec((1, 1, 1, tm), per_e), pl.BlockSpec((1, 1, 1, tm), per_e),
                  wsp((d, f)), wsp((d, f)), wsp((f, d)),
                  pl.BlockSpec((1, tm, d), row),
                  _mod_spec(gate6, tm)],
        out_specs=pl.BlockSpec((1, tm, d), row),
    )
    cnt_flat = cnt.transpose(0, 2, 1).reshape(-1)
    return pl.pallas_call(
        _moe_sparse_kernel,
        grid_spec=grid_spec,
        out_shape=jax.ShapeDtypeStruct((b, l, d), F32),
        name="moe_sparse",
    )(cnt_flat, h, rank, gates, w_gate, w_up, w_down, x, gate6)


def _final_kernel(x_ref, nw_ref, o_ref):
    x = x_ref[0]
    o_ref[0] = x * lax.rsqrt(jnp.mean(x * x, axis=-1, keepdims=True) + EPS) * nw_ref[...]


def final_norm(x, nw, tm):
    b, l, d = x.shape
    row = lambda i, j: (i, j, 0)
    return pl.pallas_call(
        _final_kernel,
        grid=(b, l // tm),
        in_specs=[pl.BlockSpec((1, tm, d), row), pl.BlockSpec((1, d), lambda i, j: (0, 0))],
        out_specs=pl.BlockSpec((1, tm, d), row),
        out_shape=jax.ShapeDtypeStruct((b, l, d), F32),
        name="final_norm",
    )(x, nw.reshape(1, d))


PAGES_PER_STEP = 8


def _sb_attn_sample_kernel(pt_ref, bias_ref, q_ref, *rest):
    npp = PAGES_PER_STEP
    k_refs, v_refs = rest[:npp], rest[npp:2 * npp]
    o_ref, acc_ref, car_ref = rest[2 * npp:]
    s = pl.program_id(1)
    nh = SB_HEADS

    @pl.when(s == 0)
    def _():
        acc_ref[...] = jnp.zeros_like(acc_ref)
        car_ref[...] = jnp.zeros_like(car_ref)

    pg = k_refs[0].shape[-1]
    qb = jnp.broadcast_to(q_ref[0], (nh, SB_HEAD_DIM, pg))
    bias_col = bias_ref[...]
    lane = lax.broadcasted_iota(jnp.int32, (nh, pg), 1)

    for r in range(npp):
        z = jnp.sum(k_refs[r][...] * qb, axis=1) + bias_col
        sp = _softplus(z)
        l1m = -sp
        c = l1m
        d = 1
        while d < pg:
            c = c + jnp.where(lane + d < pg, pltpu.roll(c, pg - d, 1), 0.0)
            d *= 2
        w = jnp.exp(z - sp + car_ref[...] + c - l1m)
        car_ref[...] += jnp.broadcast_to(c[:, 0:1], (nh, pg))
        acc_ref[...] += v_refs[r][...] * w[:, None, :]

    @pl.when(s == pl.num_programs(1) - 1)
    def _():
        o_ref[0] = jnp.sum(acc_ref[...], axis=-1)


def sb_attn_sample(q, cache_kt, cache_vt, layer, page_table, bias):
    n, nh, dh = q.shape
    n_pages = page_table.shape[1]
    pg = cache_kt.shape[-1]
    npp = PAGES_PER_STEP
    nsteps = n_pages // npp

    def page_spec(r):
        def imap(b, s, pt):
            return (layer, pt[b * n_pages + (n_pages - 1 - (s * npp + r))], 0, 0, 0)
        return pl.BlockSpec((None, None, nh, dh, pg), imap)

    grid_spec = pltpu.PrefetchScalarGridSpec(
        num_scalar_prefetch=1,
        grid=(n, nsteps),
        in_specs=([pl.BlockSpec((nh, 1), lambda b, s, pt: (0, 0)),
                   pl.BlockSpec((1, nh, dh, 1), lambda b, s, pt: (b, 0, 0, 0))]
                  + [page_spec(r) for r in range(npp)] * 2),
        out_specs=pl.BlockSpec((1, nh, dh), lambda b, s, pt: (b, 0, 0)),
        scratch_shapes=[pltpu.VMEM((nh, dh, pg), F32), pltpu.VMEM((nh, pg), F32)],
    )
    return pl.pallas_call(
        _sb_attn_sample_kernel,
        grid_spec=grid_spec,
        out_shape=jax.ShapeDtypeStruct((n, nh, dh), F32),
        name="sb_attn_sample",
    )(page_table.reshape(-1), bias.reshape(nh, 1), q.reshape(n, nh, dh, 1),
      *([cache_kt] * npp), *([cache_vt] * npp))


def _conv_sample_kernel(u_ref, st_ref, w_ref, b_ref, lnw_ref, lnb_ref, o_ref):
    km1 = CONV_K - 1
    w = w_ref[...]
    y = jnp.sum(st_ref[...] * w[None, :km1, :], axis=1) + u_ref[...] * w[km1:, :] + b_ref[...]
    o_ref[...] = _layernorm_silu(y, lnw_ref[...], lnb_ref[...])


def conv_sample(u, state, conv_w, conv_b, ln_w, ln_b):
    n, c = u.shape
    return pl.pallas_call(
        _conv_sample_kernel,
        out_shape=jax.ShapeDtypeStruct((n, c), F32),
        name="conv_sample",
    )(u, state, conv_w, conv_b.reshape(1, c), ln_w.reshape(1, c), ln_b.reshape(1, c))


def _s5_sample_kernel(u_ref, h0r_ref, h0i_ref, ar_ref, ai_ref, bbr_ref, bbi_ref, cr_ref, ci_ref, d_ref,
                      y_ref, hr_ref, hi_ref):
    u = u_ref[...]
    ein = functools.partial(jnp.einsum, precision=HIGHEST, preferred_element_type=F32)
    hr0, hi0 = h0r_ref[...], h0i_ref[...]
    ar, ai = ar_ref[...], ai_ref[...]
    hr = ar * hr0 - ai * hi0 + ein('gnc,gcp->gnp', u, bbr_ref[...])
    hi = ar * hi0 + ai * hr0 + ein('gnc,gcp->gnp', u, bbi_ref[...])
    hr_ref[...] = hr
    hi_ref[...] = hi
    y = ein('gnp,gpc->gnc', hr, cr_ref[...]) - ein('gnp,gpc->gnc', hi, ci_ref[...]) + d_ref[...] * u
    y_ref[...] = jax.nn.gelu(y)


def s5_sample(u, h0_re, h0_im, ar, ai, bbr, bbi, c_re, c_im, d):
    n, dm = u.shape
    g, p, c = bbr.shape
    t = lambda a: a.transpose(1, 0, 2)
    sh = jax.ShapeDtypeStruct
    y, hr, hi = pl.pallas_call(
        _s5_sample_kernel,
        out_shape=[sh((g, n, c), F32), sh((g, n, p), F32), sh((g, n, p), F32)],
        name="s5_sample",
    )(t(u.reshape(n, g, c)), t(h0_re), t(h0_im), ar[:, None, :], ai[:, None, :],
      bbr.transpose(0, 2, 1), bbi.transpose(0, 2, 1), c_re.transpose(0, 2, 1), c_im.transpose(0, 2, 1),
      d.reshape(g, 1, c))
    return t(y).reshape(n, dm), t(hr), t(hi)


def kernel(x_prompt, x_sample, c_prompt, c_sample, cache_k, cache_v, page_table, state_conv, state_ssm_re, state_ssm_im, w_ada, b_ada, norm_w, final_norm_w, w_in_mix, w_out_mix, sb_bias, conv_w, conv_b, conv_ln_w, conv_ln_b, ssm_lambda_re, ssm_lambda_im, ssm_b_re, ssm_b_im, ssm_c_re, ssm_c_im, ssm_d, ssm_log_dt, w_glu, w_router, b_router, w_gate, w_up, w_down):
    nb, l, d = x_prompt.shape
    ns = x_sample.shape[0]
    depth = w_ada.shape[0]
    n_attn = cache_k.shape[0]
    assert nb == SUBLANES and x_sample.shape[1] == 1
    tm = 512

    ada = ada_all(jnp.concatenate([c_prompt, c_sample], axis=0), w_ada, b_ada)
    ada = ada.reshape(depth, nb + ns, N_ADA, d)
    cache_kt = cache_k.transpose(0, 1, 3, 4, 2)
    cache_vt = cache_v.transpose(0, 1, 3, 4, 2)

    xp = x_prompt
    xs = x_sample.reshape(1, ns, d)
    kp_l, vp_l, ks_l, vs_l, cvp_l, cvs_l = [], [], [], [], [], []
    srp_l, sip_l, srs_l, sis_l = [], [], [], []
    for layer in range(depth):
        ap = [ada[layer, :nb, j].reshape(nb, 1, d) for j in range(N_ADA)]
        asm = [ada[layer, nb:, j].reshape(1, ns, d) for j in range(N_ADA)]
        nw = norm_w[layer]
        i = layer // 2
        if layer % 2 == 0:
            w_in_b, w_out_b = w_in_mix[i].astype(BF16), w_out_mix[i].astype(BF16)
            q, k, v, u = inproj(xp, nw[0], ap[0], ap[1], w_in_b, tm, BF16)
            oa = sb_attn_prompt(q, k, v, sb_bias[i])
            oc = conv_prompt(u, conv_w[i], conv_b[i], conv_ln_w[i], conv_ln_b[i])
            xp = outproj(xp, oa, oc, w_out_b, ap[2], tm)
            kp_l.append(k.reshape(nb, l, SB_HEADS, SB_HEAD_DIM))
            vp_l.append(v.reshape(nb, l, SB_HEADS, SB_HEAD_DIM))
            cvp_l.append(u[:, l - (CONV_K - 1):, :])

            qs, ks_, vs_, us = inproj(xs, nw[0], asm[0], asm[1], (w_in_mix, i), ns, F32)
            oas = sb_attn_sample(qs.reshape(ns, SB_HEADS, SB_HEAD_DIM), cache_kt, cache_vt, i, page_table, sb_bias[i])
            ocs = conv_sample(us[0], state_conv[i], conv_w[i], conv_b[i], conv_ln_w[i], conv_ln_b[i])
            xs = outproj(xs, oas.reshape(1, ns, SB_WIDTH), ocs[None], (w_out_mix, i), asm[2], ns)
            ks_l.append(ks_.reshape(ns, 1, SB_HEADS, SB_HEAD_DIM))
            vs_l.append(vs_.reshape(ns, 1, SB_HEADS, SB_HEAD_DIM))
            cvs_l.append(jnp.concatenate([state_conv[i][:, 1:, :], us.reshape(ns, 1, CONV_WIDTH)], axis=1))
        else:
            ar, ai, bbr, bbi = s5_discretize(ssm_lambda_re[i], ssm_lambda_im[i], ssm_b_re[i], ssm_b_im[i],
                                             ssm_log_dt[i])
            kb, cb, a_chunk = s5_chunk_operators(ar, ai, bbr, bbi, ssm_c_re[i], ssm_c_im[i], ssm_d[i])
            hp = normmod(xp, nw[0], ap[0], ap[1], tm, F32)
            yp, srp, sip = s5_prompt(hp, kb, cb, a_chunk)
            xp = glu_residual(xp, yp, w_glu[i].astype(BF16), ap[2], tm)
            srp_l.append(srp)
            sip_l.append(sip)

            hs = normmod(xs, nw[0], asm[0], asm[1], ns, F32)
            ys, srs, sis = s5_sample(hs[0], state_ssm_re[i], state_ssm_im[i], ar, ai, bbr, bbi,
                                     ssm_c_re[i], ssm_c_im[i], ssm_d[i])
            xs = glu_residual(xs, ys[None], (w_glu, i), asm[2], ns)
            srs_l.append(srs)
            sis_l.append(sis)
        hp, gates, rank, cnt = router(xp, nw[1], ap[3], ap[4], w_router, b_router, BF16)
        xp = moe_sparse(hp, gates, rank, cnt, w_gate[layer].astype(BF16), w_up[layer].astype(BF16),
                        w_down[layer].astype(BF16), xp, ap[5])
        hs, gates, _, _ = router(xs, nw[1], asm[3], asm[4], w_router, b_router, F32)
        xs = moe_dense(hs, gates.reshape(1, -1, ns, 1), w_gate, w_up, w_down, layer, xs, asm[5], ns)
    y_prompt = final_norm(xp, final_norm_w, tm)
    y_sample = final_norm(xs, final_norm_w, ns).reshape(ns, 1, d)
    return (y_prompt, y_sample,
            jnp.stack(kp_l), jnp.stack(vp_l), jnp.stack(ks_l), jnp.stack(vs_l),
            jnp.stack(cvp_l), jnp.stack(cvs_l),
            jnp.stack(srp_l), jnp.stack(sip_l), jnp.stack(srs_l), jnp.stack(sis_l))
```

```python
import functools
import math

import jax
import jax.numpy as jnp
from jax import lax
from jax.experimental import pallas as pl
from jax.experimental.pallas import tpu as pltpu

F32 = jnp.float32
BF16 = jnp.bfloat16
HIGHEST = lax.Precision.HIGHEST

EPS = 1e-6
LN_EPS = 1e-5
SB_HEADS = 8
SB_HEAD_DIM = 64
SB_WIDTH = SB_HEADS * SB_HEAD_DIM
CONV_WIDTH = 512
CONV_K = 31
SSM_GROUP = 16
SSM_STATE = 64
N_EXPERTS = 16
EXPERTS_PER_GROUP = 4
N_ADA = 6

LANES = 128
SUBLANES = 8
S5_CHUNK = SUBLANES
NT_DIMS = (((1,), (1,)), ((), ()))


def _dot(a, b):
    return jnp.dot(a, b, preferred_element_type=F32)


def _dotf(a, b):
    return jnp.dot(a, b, preferred_element_type=F32, precision=HIGHEST)


def _mm(a, w):
    if w.dtype == BF16:
        return _dot(a.astype(BF16), w)
    return _dotf(a, w)


def _silu(x):
    return x * jax.nn.sigmoid(x)


def _rms_mod(x, nw, shift, scale):
    y = x * lax.rsqrt(jnp.mean(x * x, axis=-1, keepdims=True) + EPS) * nw
    return y * (1 + scale) + shift


def _mod_spec(a, tm):
    d = a.shape[-1]
    if a.shape[1] == 1:
        return pl.BlockSpec((1, 1, d), lambda i, j, *_: (i, 0, 0))
    return pl.BlockSpec((1, tm, d), lambda i, j, *_: (i, j, 0))


def _weight_spec(w):
    if isinstance(w, tuple):
        stack, layer = w
        return pl.BlockSpec((None,) + stack.shape[1:], lambda i, j: (layer, 0, 0))
    return pl.BlockSpec(w.shape, lambda i, j: (0, 0))


def _weight_arr(w):
    return w[0] if isinstance(w, tuple) else w


def _split_bf16(x):
    hi = x.astype(BF16)
    lo = (x - hi.astype(F32)).astype(BF16)
    return hi, lo


def _ada_kernel(c_ref, w_ref, b_ref, o_ref):
    o_ref[0] = _dotf(_silu(c_ref[...]), w_ref[0]) + b_ref[0]


def ada_all(c, w_ada, b_ada):
    depth, d, n = w_ada.shape
    r = c.shape[0]
    tn = 1024
    return pl.pallas_call(
        _ada_kernel,
        grid=(depth, n // tn),
        in_specs=[pl.BlockSpec((r, d), lambda l, j: (0, 0)),
                  pl.BlockSpec((1, d, tn), lambda l, j: (l, 0, j)),
                  pl.BlockSpec((1, 1, tn), lambda l, j: (l, 0, j))],
        out_specs=pl.BlockSpec((1, r, tn), lambda l, j: (l, 0, j)),
        out_shape=jax.ShapeDtypeStruct((depth, r, n), F32),
        name="ada",
    )(c, w_ada, b_ada.reshape(depth, 1, n))


def _inproj_kernel(x_ref, nw_ref, sh_ref, sc_ref, w_ref, q_ref, k_ref, v_ref, u_ref):
    h = _rms_mod(x_ref[0], nw_ref[...], sh_ref[0], sc_ref[0])
    p = _mm(h, w_ref[...])
    w = SB_WIDTH
    q_ref[0] = (p[:, :w] * (1.0 / math.sqrt(SB_HEAD_DIM))).astype(q_ref.dtype)
    k_ref[0] = p[:, w:2 * w]
    v_ref[0] = p[:, 2 * w:3 * w]
    a = p[:, 3 * w:3 * w + CONV_WIDTH]
    g = p[:, 3 * w + CONV_WIDTH:]
    u_ref[0] = a * jax.nn.sigmoid(g)


def inproj(x, nw, shift, scale, w_in, tm, q_dtype):
    b, l, d = x.shape
    row = lambda i, j: (i, j, 0)
    o = jax.ShapeDtypeStruct((b, l, SB_WIDTH), F32)
    return pl.pallas_call(
        _inproj_kernel,
        grid=(b, l // tm),
        in_specs=[pl.BlockSpec((1, tm, d), row),
                  pl.BlockSpec((1, d), lambda i, j: (0, 0)),
                  _mod_spec(shift, tm), _mod_spec(scale, tm),
                  _weight_spec(w_in)],
        out_specs=[pl.BlockSpec((1, tm, SB_WIDTH), row)] * 4,
        out_shape=[jax.ShapeDtypeStruct((b, l, SB_WIDTH), q_dtype), o, o, o],
        name="inproj",
    )(x, nw.reshape(1, d), shift, scale, _weight_arr(w_in))


ATT_T = 512
ATT_SUB = LANES


def _softplus(z):
    return jnp.maximum(z, 0.0) + jnp.log(1.0 + jnp.exp(-jnp.abs(z)))


def _sb_attn_kernel(bias_ref, q_ref, k_ref, v_ref, o_ref, acc_ref, run_ref):
    t, sub = ATT_T, ATT_SUB
    hp = pl.program_id(1)
    i = pl.program_id(2)
    q2 = q_ref[0]
    lane = lax.broadcasted_iota(jnp.int32, (t, LANES), 1)
    head_of_lane = lane // SB_HEAD_DIM
    qm = [jnp.where(head_of_lane == hh, q2, jnp.zeros_like(q2)) for hh in range(2)]
    bias = [bias_ref[2 * hp + hh] for hh in range(2)]
    r_io = lax.broadcasted_iota(jnp.int32, (sub, 2 * sub), 0)
    c_io = lax.broadcasted_iota(jnp.int32, (sub, 2 * sub), 1)
    tri_ones = ((r_io > c_io) | (c_io >= sub)).astype(BF16)
    acc_ref[...] = jnp.zeros_like(acc_ref)
    run_ref[...] = jnp.zeros_like(run_ref)
    causal = lax.broadcasted_iota(jnp.int32, (t, t), 1) < lax.broadcasted_iota(jnp.int32, (t, t), 0)

    def chunk(j, masked):
        start = pl.multiple_of(j * t, t)
        kb = k_ref[0, pl.ds(start, t), :].astype(BF16)
        vb = v_ref[0, pl.ds(start, t), :].astype(BF16)
        for hh in range(2):
            z = lax.dot_general(qm[hh], kb, NT_DIMS, preferred_element_type=F32) + bias[hh]
            run = run_ref[hh]
            ws = []
            for blk in range(t // sub - 1, -1, -1):
                cols = slice(blk * sub, (blk + 1) * sub)
                r0 = blk * sub if masked else 0
                zb = z[r0:, cols]
                sp = _softplus(zb)
                if masked:
                    sp = jnp.where(causal[r0:, cols], sp, 0.0)
                sums = _dot(sp.astype(BF16), tri_ones)
                w = jnp.exp(zb - sp - (run[r0:] + sums[:, :sub]))
                if masked:
                    w = jnp.where(causal[r0:, cols], w, 0.0)
                w = w.astype(BF16)
                later = run[r0:] + sums[:, sub:]
                if r0:
                    w = jnp.concatenate([jnp.zeros((r0, sub), BF16), w], axis=0)
                    later = jnp.concatenate([run[:r0], later], axis=0)
                ws.append(w)
                run = later
            run_ref[hh] = run
            acc_ref[hh] += _dot(jnp.concatenate(ws[::-1], axis=1), vb)

    chunk(i, True)

    def body(n, c):
        chunk(i - 1 - n, False)
        return c

    lax.fori_loop(0, i, body, 0)
    o_ref[0] = jnp.where(head_of_lane == 0, acc_ref[0], acc_ref[1]).astype(o_ref.dtype)


def sb_attn_prompt(q, k, v, bias):
    b, l, w = q.shape
    t = ATT_T
    return pl.pallas_call(
        _sb_attn_kernel,
        grid=(b, w // LANES, l // t),
        in_specs=[pl.BlockSpec(memory_space=pltpu.SMEM),
                  pl.BlockSpec((1, t, LANES), lambda bi, hp, i: (bi, i, hp)),
                  pl.BlockSpec((1, l, LANES), lambda bi, hp, i: (bi, 0, hp)),
                  pl.BlockSpec((1, l, LANES), lambda bi, hp, i: (bi, 0, hp))],
        out_specs=pl.BlockSpec((1, t, LANES), lambda bi, hp, i: (bi, i, hp)),
        out_shape=jax.ShapeDtypeStruct((b, l, w), BF16),
        scratch_shapes=[pltpu.VMEM((2, t, LANES), F32), pltpu.VMEM((2, t, ATT_SUB), F32)],
        name="sb_attn_prompt",
    )(bias, q, k, v)


CONV_TR = 32
CONV_PAD = 32


def _layernorm_silu(y, lnw, lnb):
    mu = jnp.mean(y, axis=-1, keepdims=True)
    yc = y - mu
    var = jnp.mean(yc * yc, axis=-1, keepdims=True)
    return _silu(yc * lax.rsqrt(var + LN_EPS) * lnw + lnb)


def _conv_kernel(u_ref, w_ref, b_ref, lnw_ref, lnb_ref, o_ref, full_ref):
    l = u_ref.shape[1]
    tr = CONV_TR
    full_ref[0:CONV_PAD, :] = jnp.zeros((CONV_PAD, CONV_WIDTH), F32)
    full_ref[CONV_PAD:CONV_PAD + l, :] = u_ref[0]
    full_ref[CONV_PAD + l:, :] = jnp.zeros((SUBLANES, CONV_WIDTH), F32)
    first = CONV_PAD - (CONV_K - 1)
    bias = b_ref[...]
    lnw = lnw_ref[...]
    lnb = lnb_ref[...]

    def tile(t, c):
        r0 = pl.multiple_of(t * tr, tr)
        y = jnp.zeros((tr, CONV_WIDTH), F32)
        for r in range(SUBLANES):
            g = None
            for m in range(first, first + CONV_K):
                if m % SUBLANES != r:
                    continue
                win = full_ref[pl.ds(r0 + (m - r), tr + SUBLANES), :]
                term = w_ref[m - first:m - first + 1, :] * win
                g = term if g is None else g + term
            if g is not None:
                y = y + g[r:r + tr, :]
        o_ref[0, pl.ds(r0, tr), :] = _layernorm_silu(y + bias, lnw, lnb).astype(o_ref.dtype)
        return c

    lax.fori_loop(0, l // tr, tile, 0)


def conv_prompt(u, conv_w, conv_b, ln_w, ln_b):
    b, l, c = u.shape
    vec = lambda i: (0, 0)
    return pl.pallas_call(
        _conv_kernel,
        grid=(b,),
        in_specs=[pl.BlockSpec((1, l, c), lambda i: (i, 0, 0)),
                  pl.BlockSpec((CONV_K, c), vec),
                  pl.BlockSpec((1, c), vec), pl.BlockSpec((1, c), vec), pl.BlockSpec((1, c), vec)],
        out_specs=pl.BlockSpec((1, l, c), lambda i: (i, 0, 0)),
        out_shape=jax.ShapeDtypeStruct((b, l, c), BF16),
        scratch_shapes=[pltpu.VMEM((CONV_PAD + l + SUBLANES, c), F32)],
        name="conv_prompt",
    )(u, conv_w, conv_b.reshape(1, c), ln_w.reshape(1, c), ln_b.reshape(1, c))


def _outproj_kernel(x_ref, oa_ref, oc_ref, w_ref, g_ref, o_ref):
    m = _mm(oa_ref[0], w_ref[:SB_WIDTH, :]) + _mm(oc_ref[0], w_ref[SB_WIDTH:, :])
    o_ref[0] = x_ref[0] + g_ref[0] * m


def outproj(x, oa, oc, w_out, gate, tm):
    b, l, d = x.shape
    row = lambda i, j: (i, j, 0)
    return pl.pallas_call(
        _outproj_kernel,
        grid=(b, l // tm),
        in_specs=[pl.BlockSpec((1, tm, d), row),
                  pl.BlockSpec((1, tm, SB_WIDTH), row),
                  pl.BlockSpec((1, tm, CONV_WIDTH), row),
                  _weight_spec(w_out),
                  _mod_spec(gate, tm)],
        out_specs=pl.BlockSpec((1, tm, d), row),
        out_shape=jax.ShapeDtypeStruct((b, l, d), F32),
        name="outproj",
    )(x, oa, oc, _weight_arr(w_out), gate)


def _normmod_kernel(x_ref, nw_ref, sh_ref, sc_ref, o_ref):
    o_ref[0] = _rms_mod(x_ref[0], nw_ref[...], sh_ref[0], sc_ref[0]).astype(o_ref.dtype)


def normmod(x, nw, shift, scale, tm, dtype):
    b, l, d = x.shape
    row = lambda i, j: (i, j, 0)
    return pl.pallas_call(
        _normmod_kernel,
        grid=(b, l // tm),
        in_specs=[pl.BlockSpec((1, tm, d), row), pl.BlockSpec((1, d), lambda i, j: (0, 0)),
                  _mod_spec(shift, tm), _mod_spec(scale, tm)],
        out_specs=pl.BlockSpec((1, tm, d), row),
        out_shape=jax.ShapeDtypeStruct((b, l, d), dtype),
        name="normmod",
    )(x, nw.reshape(1, d), shift, scale)


def _cmul(ar, ai, br, bi):
    return ar * br - ai * bi, ar * bi + ai * br


def s5_discretize(lam_re, lam_im, b_re, b_im, log_dt):
    dt = jnp.exp(log_dt)[:, None]
    mag = jnp.exp(lam_re * dt)
    ar, ai = mag * jnp.cos(lam_im * dt), mag * jnp.sin(lam_im * dt)
    den = lam_re * lam_re + lam_im * lam_im
    fr = ((ar - 1) * lam_re + ai * lam_im) / den
    fi = (ai * lam_re - (ar - 1) * lam_im) / den
    bbr, bbi = _cmul(fr[..., None], fi[..., None], b_re, b_im)
    return ar, ai, bbr, bbi


def s5_chunk_operators(ar, ai, bbr, bbi, c_re, c_im, d):
    g, p, c = bbr.shape
    ch = S5_CHUNK
    gl = LANES // c
    nt = g // gl
    pr, pi = [jnp.ones_like(ar)], [jnp.zeros_like(ai)]
    for _ in range(ch):
        nr, ni = _cmul(pr[-1], pi[-1], ar, ai)
        pr.append(nr)
        pi.append(ni)
    pr, pi = jnp.stack(pr), jnp.stack(pi)
    car, cai = _cmul(c_re[None], c_im[None], pr[:, :, None, :], pi[:, :, None, :])
    ein = functools.partial(jnp.einsum, precision=HIGHEST)
    ktau = ein('tgcp,gpd->tgcd', car[:ch], bbr) - ein('tgcp,gpd->tgcd', cai[:ch], bbi)
    ktau = ktau.at[0].add(jnp.eye(c, dtype=F32)[None] * d.reshape(g, c)[:, :, None])
    eye = jnp.eye(gl, dtype=F32)
    kblk = (ktau.reshape(ch, nt, gl, c, c).transpose(0, 1, 2, 4, 3)[:, :, :, :, None, :]
            * eye[None, None, :, None, :, None]).reshape(ch, nt, LANES, LANES)
    s_idx = jnp.arange(ch)
    tau = s_idx[None, :] - s_idx[:, None]
    lag = (tau[None] == s_idx[:, None, None]).astype(F32)
    kbig = ein('xst,xjab->jsatb', lag, kblk).reshape(nt, ch * LANES, ch * LANES)

    def state_cols(w):
        w = w.reshape(ch, nt, gl, p, c).transpose(1, 0, 2, 4, 3)[:, :, :, :, None, :]
        return (w * eye[None, None, :, None, :, None]).reshape(nt, ch * LANES, gl * p)

    rev = ch - 1 - s_idx
    wr, wi = _cmul(pr[rev][..., None], pi[rev][..., None], bbr[None], bbi[None])
    bst = jnp.concatenate([state_cols(wr), state_cols(wi)], axis=-1)

    def state_rows(w):
        w = w.reshape(ch, nt, gl, c, p).transpose(1, 4, 0, 2, 3)[:, None]
        w = w * eye[None, :, None, None, :, None]
        return w.reshape(nt, gl * p, ch * LANES)

    cb = jnp.concatenate([state_rows(car[1:]), -state_rows(cai[1:])], axis=1)
    pr8, pi8 = pr[ch].reshape(nt, gl * p), pi[ch].reshape(nt, gl * p)
    a_chunk = jnp.stack([jnp.concatenate([pr8, pr8], -1), jnp.concatenate([-pi8, pi8], -1)], axis=1)
    return kbig.astype(BF16), bst.astype(BF16), cb.astype(BF16), a_chunk


def _s5_kernel(u_ref, kbig_ref, bst_ref, cb_ref, a_ref, y_ref, hl_ref, s_scr, y_scr):
    ch = S5_CHUNK
    nk = u_ref.shape[1] // ch
    ucat = jnp.concatenate([u_ref[0, pl.ds(s, nk, stride=ch), :].astype(BF16) for s in range(ch)], axis=1)
    s_scr[...] = _dot(ucat, bst_ref[0])
    a_same = a_ref[0, 0:1, :]
    a_swap = a_ref[0, 1:2, :]
    half = a_same.shape[-1] // 2

    def rows(gi, h):
        base = pl.multiple_of(gi * SUBLANES, SUBLANES)
        blk = s_scr[pl.ds(base, SUBLANES), :]
        before = []
        for j in range(SUBLANES):
            before.append(h)
            h = h * a_same + pltpu.roll(h, half, 1) * a_swap + blk[j:j + 1, :]
        s_scr[pl.ds(base, SUBLANES), :] = jnp.concatenate(before, axis=0)
        return h

    hl_ref[0, 0] = lax.fori_loop(0, nk // SUBLANES, rows, jnp.zeros((1, 2 * half), F32))
    y = _dot(ucat, kbig_ref[0]) + _dot(s_scr[...].astype(BF16), cb_ref[0])
    for t in range(ch):
        y_scr[pl.ds(t, nk, stride=ch), :] = jax.nn.gelu(y[:, t * LANES:(t + 1) * LANES])
    y_ref[0] = y_scr[...].astype(y_ref.dtype)


def s5_prompt(u, kbig, bst, cb, a_chunk):
    b, l, dm = u.shape
    nt = kbig.shape[0]
    ch = S5_CHUNK
    sw = cb.shape[1]
    tile = lambda a: pl.BlockSpec((1,) + a.shape[1:], lambda j, i: (j, 0, 0))
    y, hl = pl.pallas_call(
        _s5_kernel,
        grid=(nt, b),
        in_specs=[pl.BlockSpec((1, l, LANES), lambda j, i: (i, 0, j)),
                  tile(kbig), tile(bst), tile(cb), tile(a_chunk)],
        out_specs=[pl.BlockSpec((1, l, LANES), lambda j, i: (i, 0, j)),
                   pl.BlockSpec((1, 1, 1, sw), lambda j, i: (i, j, 0, 0))],
        out_shape=[jax.ShapeDtypeStruct((b, l, dm), BF16), jax.ShapeDtypeStruct((b, nt, 1, sw), F32)],
        scratch_shapes=[pltpu.VMEM((l // ch, sw), F32), pltpu.VMEM((l, LANES), F32)],
        name="s5_prompt",
    )(u, kbig, bst, cb, a_chunk)
    hl = hl.reshape(b, nt, 2, -1, SSM_STATE)
    return y, hl[:, :, 0].reshape(b, -1, SSM_STATE), hl[:, :, 1].reshape(b, -1, SSM_STATE)


def _glu_kernel(x_ref, y_ref, w_ref, g_ref, o_ref):
    z = _mm(y_ref[0], w_ref[...])
    d = o_ref.shape[-1]
    o_ref[0] = x_ref[0] + g_ref[0] * (z[:, :d] * jax.nn.sigmoid(z[:, d:]))


def glu_residual(x, y, w_glu, gate, tm):
    b, l, d = x.shape
    row = lambda i, j: (i, j, 0)
    return pl.pallas_call(
        _glu_kernel,
        grid=(b, l // tm),
        in_specs=[pl.BlockSpec((1, tm, d), row), pl.BlockSpec((1, tm, d), row),
                  _weight_spec(w_glu),
                  _mod_spec(gate, tm)],
        out_specs=pl.BlockSpec((1, tm, d), row),
        out_shape=jax.ShapeDtypeStruct((b, l, d), F32),
        name="glu_residual",
    )(x, y, _weight_arr(w_glu), gate)


def _route(logits_t, b_col):
    e_tot = logits_t.shape[0]
    epg = EXPERTS_PER_GROUP
    ng = e_tot // epg
    scores = jax.nn.sigmoid(logits_t)
    sel_all = scores + b_col
    sel = [sel_all[e:e + 1, :] for e in range(e_tot)]
    sc = [scores[e:e + 1, :] for e in range(e_tot)]
    gsum = []
    for g in range(ng):
        a, b, c, d = sel[epg * g:epg * g + epg]
        hi1, lo1, hi2, lo2 = jnp.maximum(a, b), jnp.minimum(a, b), jnp.maximum(c, d), jnp.minimum(c, d)
        gsum.append(jnp.maximum(hi1, hi2) + jnp.maximum(jnp.minimum(hi1, hi2), jnp.maximum(lo1, lo2)))
    best = gsum[0]
    gi = jnp.zeros_like(best, dtype=jnp.int32)
    for g in range(1, ng):
        upd = gsum[g] > best
        gi = jnp.where(upd, g, gi)
        best = jnp.where(upd, gsum[g], best)
    picked = []
    for e in range(e_tot):
        g, j = divmod(e, epg)
        ahead = jnp.zeros_like(gi)
        for j2 in range(epg):
            if j2 == j:
                continue
            o = sel[epg * g + j2]
            before = (o >= sel[e]) if j2 < j else (o > sel[e])
            ahead = ahead + before.astype(jnp.int32)
        picked.append((gi == g) & (ahead < 2))
    wk = [jnp.where(picked[e], sc[e], 0.0) for e in range(e_tot)]
    tot = wk[0]
    for e in range(1, e_tot):
        tot = tot + wk[e]
    return jnp.concatenate([w / tot for w in wk], axis=0)


MOE_SUB = 512


def _router_kernel(x_ref, nw_ref, sh_ref, sc_ref, wr_ref, br_ref, h_ref, g_ref, rank_ref, cnt_ref):
    h = _rms_mod(x_ref[0], nw_ref[...], sh_ref[0], sc_ref[0])
    h_ref[0] = h.astype(h_ref.dtype)
    logits_t = lax.dot_general(wr_ref[...], h, NT_DIMS, preferred_element_type=F32, precision=HIGHEST)
    gates = _route(logits_t, br_ref[...])
    e_tot, l = gates.shape
    sub = min(MOE_SUB, l)
    picked = gates > 0.0
    within = lax.broadcasted_iota(jnp.int32, (e_tot, l), 1) % sub
    c = picked.astype(F32)
    d = 1
    while d < sub:
        c = c + jnp.where(within >= d, pltpu.roll(c, d, 1), 0.0)
        d *= 2
    rank = jnp.where(picked, c - 1.0, -1.0).astype(jnp.int32)
    for e in range(e_tot):
        g_ref[0, e] = gates[e:e + 1, :]
        rank_ref[0, e] = rank[e:e + 1, :]
    cnt_ref[0] = jnp.concatenate([c[:, s * sub + sub - 1:s * sub + sub] for s in range(l // sub)],
                                 axis=1).astype(jnp.int32)


def router(x, nw, shift, scale, w_router, b_router, h_dtype):
    b, l, d = x.shape
    e = w_router.shape[1]
    nsub = l // min(MOE_SUB, l)
    row = lambda i: (i, 0, 0)
    const = lambda i: (0, 0)
    mod = lambda a: pl.BlockSpec((1,) + a.shape[1:], row)
    return pl.pallas_call(
        _router_kernel,
        grid=(b,),
        in_specs=[pl.BlockSpec((1, l, d), row), pl.BlockSpec((1, d), const), mod(shift), mod(scale),
                  pl.BlockSpec((e, d), const), pl.BlockSpec((e, 1), const)],
        out_specs=[pl.BlockSpec((1, l, d), row),
                   pl.BlockSpec((1, e, 1, l), lambda i: (i, 0, 0, 0)),
                   pl.BlockSpec((1, e, 1, l), lambda i: (i, 0, 0, 0)),
                   pl.BlockSpec((1, e, nsub), row)],
        out_shape=[jax.ShapeDtypeStruct((b, l, d), h_dtype), jax.ShapeDtypeStruct((b, e, 1, l), F32),
                   jax.ShapeDtypeStruct((b, e, 1, l), jnp.int32), jax.ShapeDtypeStruct((b, e, nsub), jnp.int32)],
        name="router",
    )(x, nw.reshape(1, d), shift, scale, w_router.T, b_router.reshape(e, 1))


def _moe_dense_kernel(h_ref, gt_ref, wg_ref, wu_ref, wd_ref, x_ref, g6_ref, o_ref, acc_ref):
    e = pl.program_id(2)

    @pl.when(e == 0)
    def _():
        acc_ref[...] = jnp.zeros_like(acc_ref)

    h = h_ref[0]
    hg = _mm(h, wg_ref[0])
    hu = _mm(h, wu_ref[0])
    act = _silu(hg) * hu * gt_ref[0, 0]
    acc_ref[...] += _mm(act, wd_ref[0])

    @pl.when(e == pl.num_programs(2) - 1)
    def _():
        o_ref[0] = x_ref[0] + g6_ref[0] * acc_ref[...]


def moe_dense(h, gates_col, w_gate, w_up, w_down, layer, x, gate6, tm):
    b, l, d = x.shape
    _, e, _, f = w_gate.shape
    row = lambda i, j, k: (i, j, 0)
    wspec = lambda s: pl.BlockSpec((None, 1) + s, lambda i, j, k: (layer, k, 0, 0))
    return pl.pallas_call(
        _moe_dense_kernel,
        grid=(b, l // tm, e),
        in_specs=[pl.BlockSpec((1, tm, d), row),
                  pl.BlockSpec((1, 1, tm, 1), lambda i, j, k: (i, k, j, 0)),
                  wspec((d, f)), wspec((d, f)), wspec((f, d)),
                  pl.BlockSpec((1, tm, d), row),
                  _mod_spec(gate6, tm)],
        out_specs=pl.BlockSpec((1, tm, d), row),
        out_shape=jax.ShapeDtypeStruct((b, l, d), F32),
        scratch_shapes=[pltpu.VMEM((tm, d), F32)],
        name="moe_dense",
    )(h, gates_col, w_gate, w_up, w_down, x, gate6)


MOE_CAP = 96
MOE_TM = 2048
TN_DIMS = (((0,), (0,)), ((), ()))


def _moe_sparse_kernel(cnt_ref, h_ref, rank_ref, gt_ref, wg_ref, wu_ref, wd_ref, x_ref, g6_ref, o_ref):
    bi, j, e = pl.program_id(0), pl.program_id(1), pl.program_id(2)
    tm = o_ref.shape[1]
    sub = min(MOE_SUB, tm)
    nsub = tm // sub
    cap = MOE_CAP

    @pl.when(e == 0)
    def _():
        o_ref[...] = jnp.zeros_like(o_ref)

    row_io = lax.broadcasted_iota(jnp.int32, (cap, sub), 0)

    def chunks(subs, c):
        ps, xs, gs = [], [], []
        for s in subs:
            cols = slice(s * sub, (s + 1) * sub)
            pick = rank_ref[0, 0, :, cols] == row_io + c * cap
            p = pick.astype(BF16)
            ps.append(p)
            xs.append(_dot(p, h_ref[0, cols, :]).astype(BF16))
            gs.append(jnp.sum(jnp.where(pick, gt_ref[0, 0, :, cols], 0.0), axis=1, keepdims=True))
        xg = jnp.concatenate(xs, axis=0)
        act = _silu(_dot(xg, wg_ref[0])) * _dot(xg, wu_ref[0]) * jnp.concatenate(gs, axis=0)
        y = _dot(act.astype(BF16), wd_ref[0]).astype(BF16)
        for n, s in enumerate(subs):
            o_ref[0, s * sub:(s + 1) * sub, :] += lax.dot_general(ps[n], y[n * cap:(n + 1) * cap, :],
                                                                 TN_DIMS, preferred_element_type=F32)

    chunks(list(range(nsub)), 0)
    for s in range(nsub):
        n = cnt_ref[((bi * pl.num_programs(1) + j) * nsub + s) * pl.num_programs(2) + e]

        def more(c, carry, s=s):
            chunks([s], c)
            return carry

        lax.fori_loop(1, (n + cap - 1) // cap, more, 0)

    @pl.when(e == pl.num_programs(2) - 1)
    def _():
        o_ref[0] = x_ref[0] + g6_ref[0] * o_ref[0]


def moe_sparse(h, gates, rank, cnt, w_gate, w_up, w_down, x, gate6):
    b, l, d = x.shape
    e, _, f = w_gate.shape
    tm = min(MOE_TM, l)
    row = lambda i, j, k, c: (i, j, 0)
    per_e = lambda i, j, k, c: (i, k, 0, j)
    wsp = lambda s: pl.BlockSpec((1,) + s, lambda i, j, k, c: (k, 0, 0))
    grid_spec = pltpu.PrefetchScalarGridSpec(
        num_scalar_prefetch=1,
        grid=(b, l // tm, e),
        in_specs=[pl.BlockSpec((1, tm, d), row),
                  pl.BlockSpec((1, 1, 1, tm), per_e), pl.BlockSpec((1, 1, 1, tm), per_e),
                  wsp((d, f)), wsp((d, f)), wsp((f, d)),
                  pl.BlockSpec((1, tm, d), row),
                  _mod_spec(gate6, tm)],
        out_specs=pl.BlockSpec((1, tm, d), row),
    )
    cnt_flat = cnt.transpose(0, 2, 1).reshape(-1)
    return pl.pallas_call(
        _moe_sparse_kernel,
        grid_spec=grid_spec,
        out_shape=jax.ShapeDtypeStruct((b, l, d), F32),
        name="moe_sparse",
    )(cnt_flat, h, rank, gates, w_gate, w_up, w_down, x, gate6)


def _final_kernel(x_ref, nw_ref, o_ref):
    x = x_ref[0]
    o_ref[0] = x * lax.rsqrt(jnp.mean(x * x, axis=-1, keepdims=True) + EPS) * nw_ref[...]


def final_norm(x, nw, tm):
    b, l, d = x.shape
    row = lambda i, j: (i, j, 0)
    return pl.pallas_call(
        _final_kernel,
        grid=(b, l // tm),
        in_specs=[pl.BlockSpec((1, tm, d), row), pl.BlockSpec((1, d), lambda i, j: (0, 0))],
        out_specs=pl.BlockSpec((1, tm, d), row),
        out_shape=jax.ShapeDtypeStruct((b, l, d), F32),
        name="final_norm",
    )(x, nw.reshape(1, d))


PAGES_PER_STEP = 8


def _sb_attn_sample_kernel(pt_ref, bias_ref, q_ref, *rest):
    npp = PAGES_PER_STEP
    k_refs, v_refs = rest[:npp], rest[npp:2 * npp]
    o_ref, acc_ref, car_ref = rest[2 * npp:]
    s = pl.program_id(1)
    nh = SB_HEADS

    @pl.when(s == 0)
    def _():
        acc_ref[...] = jnp.zeros_like(acc_ref)
        car_ref[...] = jnp.zeros_like(car_ref)

    pg = k_refs[0].shape[-1]
    qb = jnp.broadcast_to(q_ref[0], (nh, SB_HEAD_DIM, pg))
    bias_col = bias_ref[...]
    lane = lax.broadcasted_iota(jnp.int32, (nh, pg), 1)

    for r in range(npp):
        z = jnp.sum(k_refs[r][...] * qb, axis=1) + bias_col
        sp = _softplus(z)
        l1m = -sp
        c = l1m
        d = 1
        while d < pg:
            c = c + jnp.where(lane + d < pg, pltpu.roll(c, pg - d, 1), 0.0)
            d *= 2
        w = jnp.exp(z - sp + car_ref[...] + c - l1m)
        car_ref[...] += jnp.broadcast_to(c[:, 0:1], (nh, pg))
        acc_ref[...] += v_refs[r][...] * w[:, None, :]

    @pl.when(s == pl.num_programs(1) - 1)
    def _():
        o_ref[0] = jnp.sum(acc_ref[...], axis=-1)


def sb_attn_sample(q, cache_kt, cache_vt, layer, page_table, bias):
    n, nh, dh = q.shape
    n_pages = page_table.shape[1]
    pg = cache_kt.shape[-1]
    npp = PAGES_PER_STEP
    nsteps = n_pages // npp

    def page_spec(r):
        def imap(b, s, pt):
            return (layer, pt[b * n_pages + (n_pages - 1 - (s * npp + r))], 0, 0, 0)
        return pl.BlockSpec((None, None, nh, dh, pg), imap)

    grid_spec = pltpu.PrefetchScalarGridSpec(
        num_scalar_prefetch=1,
        grid=(n, nsteps),
        in_specs=([pl.BlockSpec((nh, 1), lambda b, s, pt: (0, 0)),
                   pl.BlockSpec((1, nh, dh, 1), lambda b, s, pt: (b, 0, 0, 0))]
                  + [page_spec(r) for r in range(npp)] * 2),
        out_specs=pl.BlockSpec((1, nh, dh), lambda b, s, pt: (b, 0, 0)),
        scratch_shapes=[pltpu.VMEM((nh, dh, pg), F32), pltpu.VMEM((nh, pg), F32)],
    )
    return pl.pallas_call(
        _sb_attn_sample_kernel,
        grid_spec=grid_spec,
        out_shape=jax.ShapeDtypeStruct((n, nh, dh), F32),
        name="sb_attn_sample",
    )(page_table.reshape(-1), bias.reshape(nh, 1), q.reshape(n, nh, dh, 1),
      *([cache_kt] * npp), *([cache_vt] * npp))


def _conv_sample_kernel(u_ref, st_ref, w_ref, b_ref, lnw_ref, lnb_ref, o_ref):
    km1 = CONV_K - 1
    w = w_ref[...]
    y = jnp.sum(st_ref[...] * w[None, :km1, :], axis=1) + u_ref[...] * w[km1:, :] + b_ref[...]
    o_ref[...] = _layernorm_silu(y, lnw_ref[...], lnb_ref[...])


def conv_sample(u, state, conv_w, conv_b, ln_w, ln_b):
    n, c = u.shape
    return pl.pallas_call(
        _conv_sample_kernel,
        out_shape=jax.ShapeDtypeStruct((n, c), F32),
        name="conv_sample",
    )(u, state, conv_w, conv_b.reshape(1, c), ln_w.reshape(1, c), ln_b.reshape(1, c))


def _s5_sample_kernel(u_ref, h0r_ref, h0i_ref, ar_ref, ai_ref, bbr_ref, bbi_ref, cr_ref, ci_ref, d_ref,
                      y_ref, hr_ref, hi_ref):
    u = u_ref[...]
    ein = functools.partial(jnp.einsum, precision=HIGHEST, preferred_element_type=F32)
    hr0, hi0 = h0r_ref[...], h0i_ref[...]
    ar, ai = ar_ref[...], ai_ref[...]
    hr = ar * hr0 - ai * hi0 + ein('gnc,gcp->gnp', u, bbr_ref[...])
    hi = ar * hi0 + ai * hr0 + ein('gnc,gcp->gnp', u, bbi_ref[...])
    hr_ref[...] = hr
    hi_ref[...] = hi
    y = ein('gnp,gpc->gnc', hr, cr_ref[...]) - ein('gnp,gpc->gnc', hi, ci_ref[...]) + d_ref[...] * u
    y_ref[...] = jax.nn.gelu(y)


def s5_sample(u, h0_re, h0_im, ar, ai, bbr, bbi, c_re, c_im, d):
    n, dm = u.shape
    g, p, c = bbr.shape
    t = lambda a: a.transpose(1, 0, 2)
    sh = jax.ShapeDtypeStruct
    y, hr, hi = pl.pallas_call(
        _s5_sample_kernel,
        out_shape=[sh((g, n, c), F32), sh((g, n, p), F32), sh((g, n, p), F32)],
        name="s5_sample",
    )(t(u.reshape(n, g, c)), t(h0_re), t(h0_im), ar[:, None, :], ai[:, None, :],
      bbr.transpose(0, 2, 1), bbi.transpose(0, 2, 1), c_re.transpose(0, 2, 1), c_im.transpose(0, 2, 1),
      d.reshape(g, 1, c))
    return t(y).reshape(n, dm), t(hr), t(hi)


def kernel(x_prompt, x_sample, c_prompt, c_sample, cache_k, cache_v, page_table, state_conv, state_ssm_re, state_ssm_im, w_ada, b_ada, norm_w, final_norm_w, w_in_mix, w_out_mix, sb_bias, conv_w, conv_b, conv_ln_w, conv_ln_b, ssm_lambda_re, ssm_lambda_im, ssm_b_re, ssm_b_im, ssm_c_re, ssm_c_im, ssm_d, ssm_log_dt, w_glu, w_router, b_router, w_gate, w_up, w_down):
    nb, l, d = x_prompt.shape
    ns = x_sample.shape[0]
    depth = w_ada.shape[0]
    n_attn = cache_k.shape[0]
    assert nb == SUBLANES and x_sample.shape[1] == 1
    tm = 512

    ada = ada_all(jnp.concatenate([c_prompt, c_sample], axis=0), w_ada, b_ada)
    ada = ada.reshape(depth, nb + ns, N_ADA, d)
    cache_kt = cache_k.transpose(0, 1, 3, 4, 2)
    cache_vt = cache_v.transpose(0, 1, 3, 4, 2)

    xp = x_prompt
    xs = x_sample.reshape(1, ns, d)
    kp_l, vp_l, ks_l, vs_l, cvp_l, cvs_l = [], [], [], [], [], []
    srp_l, sip_l, srs_l, sis_l = [], [], [], []
    for layer in range(depth):
        ap = [ada[layer, :nb, j].reshape(nb, 1, d) for j in range(N_ADA)]
        asm = [ada[layer, nb:, j].reshape(1, ns, d) for j in range(N_ADA)]
        nw = norm_w[layer]
        i = layer // 2
        if layer % 2 == 0:
            w_in_b, w_out_b = w_in_mix[i].astype(BF16), w_out_mix[i].astype(BF16)
            q, k, v, u = inproj(xp, nw[0], ap[0], ap[1], w_in_b, tm, BF16)
            oa = sb_attn_prompt(q, k, v, sb_bias[i])
            oc = conv_prompt(u, conv_w[i], conv_b[i], conv_ln_w[i], conv_ln_b[i])
            xp = outproj(xp, oa, oc, w_out_b, ap[2], tm)
            kp_l.append(k.reshape(nb, l, SB_HEADS, SB_HEAD_DIM))
            vp_l.append(v.reshape(nb, l, SB_HEADS, SB_HEAD_DIM))
            cvp_l.append(u[:, l - (CONV_K - 1):, :])

            qs, ks_, vs_, us = inproj(xs, nw[0], asm[0], asm[1], (w_in_mix, i), ns, F32)
            oas = sb_attn_sample(qs.reshape(ns, SB_HEADS, SB_HEAD_DIM), cache_kt, cache_vt, i, page_table, sb_bias[i])
            ocs = conv_sample(us[0], state_conv[i], conv_w[i], conv_b[i], conv_ln_w[i], conv_ln_b[i])
            xs = outproj(xs, oas.reshape(1, ns, SB_WIDTH), ocs[None], (w_out_mix, i), asm[2], ns)
            ks_l.append(ks_.reshape(ns, 1, SB_HEADS, SB_HEAD_DIM))
            vs_l.append(vs_.reshape(ns, 1, SB_HEADS, SB_HEAD_DIM))
            cvs_l.append(jnp.concatenate([state_conv[i][:, 1:, :], us.reshape(ns, 1, CONV_WIDTH)], axis=1))
        else:
            ar, ai, bbr, bbi = s5_discretize(ssm_lambda_re[i], ssm_lambda_im[i], ssm_b_re[i], ssm_b_im[i],
                                             ssm_log_dt[i])
            s5_ops = s5_chunk_operators(ar, ai, bbr, bbi, ssm_c_re[i], ssm_c_im[i], ssm_d[i])
            hp = normmod(xp, nw[0], ap[0], ap[1], tm, F32)
            yp, srp, sip = s5_prompt(hp, *s5_ops)
            xp = glu_residual(xp, yp, w_glu[i].astype(BF16), ap[2], tm)
            srp_l.append(srp)
            sip_l.append(sip)

            hs = normmod(xs, nw[0], asm[0], asm[1], ns, F32)
            ys, srs, sis = s5_sample(hs[0], state_ssm_re[i], state_ssm_im[i], ar, ai, bbr, bbi,
                                     ssm_c_re[i], ssm_c_im[i], ssm_d[i])
            xs = glu_residual(xs, ys[None], (w_glu, i), asm[2], ns)
            srs_l.append(srs)
            sis_l.append(sis)
        hp, gates, rank, cnt = router(xp, nw[1], ap[3], ap[4], w_router, b_router, BF16)
        xp = moe_sparse(hp, gates, rank, cnt, w_gate[layer].astype(BF16), w_up[layer].astype(BF16),
                        w_down[layer].astype(BF16), xp, ap[5])
        hs, gates, _, _ = router(xs, nw[1], asm[3], asm[4], w_router, b_router, F32)
        xs = moe_dense(hs, gates.reshape(1, -1, ns, 1), w_gate, w_up, w_down, layer, xs, asm[5], ns)
    y_prompt = final_norm(xp, final_norm_w, tm)
    y_sample = final_norm(xs, final_norm_w, ns).reshape(ns, 1, d)
    return (y_prompt, y_sample,
            jnp.stack(kp_l), jnp.stack(vp_l), jnp.stack(ks_l), jnp.stack(vs_l),
            jnp.stack(cvp_l), jnp.stack(cvs_l),
            jnp.stack(srp_l), jnp.stack(sip_l), jnp.stack(srs_l), jnp.stack(sis_l))
```

```python
import functools
import math

import jax
import jax.numpy as jnp
from jax import lax
from jax.experimental import pallas as pl
from jax.experimental.pallas import tpu as pltpu

F32 = jnp.float32
BF16 = jnp.bfloat16
HIGHEST = lax.Precision.HIGHEST

EPS = 1e-6
LN_EPS = 1e-5
SB_HEADS = 8
SB_HEAD_DIM = 64
SB_WIDTH = SB_HEADS * SB_HEAD_DIM
CONV_WIDTH = 512
CONV_K = 31
SSM_GROUP = 16
SSM_STATE = 64
N_EXPERTS = 16
EXPERTS_PER_GROUP = 4
N_ADA = 6

LANES = 128
SUBLANES = 8
S5_CHUNK = SUBLANES
NT_DIMS = (((1,), (1,)), ((), ()))


def _dot(a, b):
    return jnp.dot(a, b, preferred_element_type=F32)


def _dotf(a, b):
    return jnp.dot(a, b, preferred_element_type=F32, precision=HIGHEST)


def _mm(a, w):
    if w.dtype == BF16:
        return _dot(a.astype(BF16), w)
    return _dotf(a, w)


def _silu(x):
    return x * jax.nn.sigmoid(x)


def _rms_mod(x, nw, shift, scale):
    y = x * lax.rsqrt(jnp.mean(x * x, axis=-1, keepdims=True) + EPS) * nw
    return y * (1 + scale) + shift


def _mod_spec(a, tm):
    d = a.shape[-1]
    if a.shape[1] == 1:
        return pl.BlockSpec((1, 1, d), lambda i, j, *_: (i, 0, 0))
    return pl.BlockSpec((1, tm, d), lambda i, j, *_: (i, j, 0))


def _weight_spec(w):
    if isinstance(w, tuple):
        stack, layer = w
        return pl.BlockSpec((None,) + stack.shape[1:], lambda i, j: (layer, 0, 0))
    return pl.BlockSpec(w.shape, lambda i, j: (0, 0))


def _weight_arr(w):
    return w[0] if isinstance(w, tuple) else w


def _split_bf16(x):
    hi = x.astype(BF16)
    lo = (x - hi.astype(F32)).astype(BF16)
    return hi, lo


def _ada_kernel(c_ref, w_ref, b_ref, o_ref):
    o_ref[0] = _dotf(_silu(c_ref[...]), w_ref[0]) + b_ref[0]


def ada_all(c, w_ada, b_ada):
    depth, d, n = w_ada.shape
    r = c.shape[0]
    tn = 1024
    return pl.pallas_call(
        _ada_kernel,
        grid=(depth, n // tn),
        in_specs=[pl.BlockSpec((r, d), lambda l, j: (0, 0)),
                  pl.BlockSpec((1, d, tn), lambda l, j: (l, 0, j)),
                  pl.BlockSpec((1, 1, tn), lambda l, j: (l, 0, j))],
        out_specs=pl.BlockSpec((1, r, tn), lambda l, j: (l, 0, j)),
        out_shape=jax.ShapeDtypeStruct((depth, r, n), F32),
        name="ada",
    )(c, w_ada, b_ada.reshape(depth, 1, n))


LOG2E = math.log2(math.e)
Q_SCALE = 1.0 / math.sqrt(SB_HEAD_DIM)


def _inproj_kernel(q_scale, x_ref, nw_ref, sh_ref, sc_ref, w_ref, q_ref, k_ref, v_ref, u_ref):
    h = _rms_mod(x_ref[0], nw_ref[...], sh_ref[0], sc_ref[0])
    p = _mm(h, w_ref[...])
    w = SB_WIDTH
    q_ref[0] = (p[:, :w] * q_scale).astype(q_ref.dtype)
    k_ref[0] = p[:, w:2 * w]
    v_ref[0] = p[:, 2 * w:3 * w]
    a = p[:, 3 * w:3 * w + CONV_WIDTH]
    g = p[:, 3 * w + CONV_WIDTH:]
    u_ref[0] = a * jax.nn.sigmoid(g)


def inproj(x, nw, shift, scale, w_in, tm, q_dtype, q_scale):
    b, l, d = x.shape
    row = lambda i, j: (i, j, 0)
    o = jax.ShapeDtypeStruct((b, l, SB_WIDTH), F32)
    return pl.pallas_call(
        functools.partial(_inproj_kernel, q_scale),
        grid=(b, l // tm),
        in_specs=[pl.BlockSpec((1, tm, d), row),
                  pl.BlockSpec((1, d), lambda i, j: (0, 0)),
                  _mod_spec(shift, tm), _mod_spec(scale, tm),
                  _weight_spec(w_in)],
        out_specs=[pl.BlockSpec((1, tm, SB_WIDTH), row)] * 4,
        out_shape=[jax.ShapeDtypeStruct((b, l, SB_WIDTH), q_dtype), o, o, o],
        name="inproj",
    )(x, nw.reshape(1, d), shift, scale, _weight_arr(w_in))


ATT_T = 512
ATT_SUB = LANES


def _softplus(z):
    return jnp.maximum(z, 0.0) + jnp.log(1.0 + jnp.exp(-jnp.abs(z)))


def _softplus2(z2):
    return jnp.maximum(z2, 0.0) + jnp.log2(1.0 + jnp.exp2(-jnp.abs(z2)))


def _sb_attn_kernel(bias_ref, q_ref, k_ref, v_ref, o_ref, acc_ref, run_ref):
    t, sub = ATT_T, ATT_SUB
    hp = pl.program_id(1)
    i = pl.program_id(2)
    q2 = q_ref[0]
    lane = lax.broadcasted_iota(jnp.int32, (t, LANES), 1)
    head_of_lane = lane // SB_HEAD_DIM
    qm = [jnp.where(head_of_lane == hh, q2, jnp.zeros_like(q2)) for hh in range(2)]
    bias = [bias_ref[2 * hp + hh] * LOG2E for hh in range(2)]
    r_io = lax.broadcasted_iota(jnp.int32, (sub, 2 * sub), 0)
    c_io = lax.broadcasted_iota(jnp.int32, (sub, 2 * sub), 1)
    tri_ones = ((r_io > c_io) | (c_io >= sub)).astype(BF16)
    acc_ref[...] = jnp.zeros_like(acc_ref)
    run_ref[...] = jnp.zeros_like(run_ref)
    causal = lax.broadcasted_iota(jnp.int32, (t, t), 1) < lax.broadcasted_iota(jnp.int32, (t, t), 0)

    def chunk(j, masked):
        start = pl.multiple_of(j * t, t)
        kb = k_ref[0, pl.ds(start, t), :].astype(BF16)
        vb = v_ref[0, pl.ds(start, t), :].astype(BF16)
        for hh in range(2):
            z = lax.dot_general(qm[hh], kb, NT_DIMS, preferred_element_type=F32) + bias[hh]
            run = run_ref[hh]
            ws = []
            for blk in range(t // sub - 1, -1, -1):
                cols = slice(blk * sub, (blk + 1) * sub)
                r0 = blk * sub if masked else 0
                zb = z[r0:, cols]
                sp = _softplus2(zb)
                if masked:
                    sp = jnp.where(causal[r0:, cols], sp, 0.0)
                sums = _dot(sp.astype(BF16), tri_ones)
                w = jnp.exp2(zb - sp - (run[r0:] + sums[:, :sub]))
                if masked:
                    w = jnp.where(causal[r0:, cols], w, 0.0)
                w = w.astype(BF16)
                later = run[r0:] + sums[:, sub:]
                if r0:
                    w = jnp.concatenate([jnp.zeros((r0, sub), BF16), w], axis=0)
                    later = jnp.concatenate([run[:r0], later], axis=0)
                ws.append(w)
                run = later
            run_ref[hh] = run
            acc_ref[hh] += _dot(jnp.concatenate(ws[::-1], axis=1), vb)

    chunk(i, True)

    def body(n, c):
        chunk(i - 1 - n, False)
        return c

    lax.fori_loop(0, i, body, 0)
    o_ref[0] = jnp.where(head_of_lane == 0, acc_ref[0], acc_ref[1]).astype(o_ref.dtype)


def sb_attn_prompt(q, k, v, bias):
    b, l, w = q.shape
    t = ATT_T
    return pl.pallas_call(
        _sb_attn_kernel,
        grid=(b, w // LANES, l // t),
        in_specs=[pl.BlockSpec(memory_space=pltpu.SMEM),
                  pl.BlockSpec((1, t, LANES), lambda bi, hp, i: (bi, i, hp)),
                  pl.BlockSpec((1, l, LANES), lambda bi, hp, i: (bi, 0, hp)),
                  pl.BlockSpec((1, l, LANES), lambda bi, hp, i: (bi, 0, hp))],
        out_specs=pl.BlockSpec((1, t, LANES), lambda bi, hp, i: (bi, i, hp)),
        out_shape=jax.ShapeDtypeStruct((b, l, w), BF16),
        scratch_shapes=[pltpu.VMEM((2, t, LANES), F32), pltpu.VMEM((2, t, ATT_SUB), F32)],
        name="sb_attn_prompt",
    )(bias, q, k, v)


CONV_TR = 32
CONV_PAD = 32


def _layernorm_silu(y, lnw, lnb):
    mu = jnp.mean(y, axis=-1, keepdims=True)
    yc = y - mu
    var = jnp.mean(yc * yc, axis=-1, keepdims=True)
    return _silu(yc * lax.rsqrt(var + LN_EPS) * lnw + lnb)


def _conv_kernel(u_ref, w_ref, b_ref, lnw_ref, lnb_ref, o_ref, full_ref):
    l = u_ref.shape[1]
    tr = CONV_TR
    full_ref[0:CONV_PAD, :] = jnp.zeros((CONV_PAD, CONV_WIDTH), F32)
    full_ref[CONV_PAD:CONV_PAD + l, :] = u_ref[0]
    full_ref[CONV_PAD + l:, :] = jnp.zeros((SUBLANES, CONV_WIDTH), F32)
    first = CONV_PAD - (CONV_K - 1)
    bias = b_ref[...]
    lnw = lnw_ref[...]
    lnb = lnb_ref[...]

    def tile(t, c):
        r0 = pl.multiple_of(t * tr, tr)
        y = jnp.zeros((tr, CONV_WIDTH), F32)
        for r in range(SUBLANES):
            g = None
            for m in range(first, first + CONV_K):
                if m % SUBLANES != r:
                    continue
                win = full_ref[pl.ds(r0 + (m - r), tr + SUBLANES), :]
                term = w_ref[m - first:m - first + 1, :] * win
                g = term if g is None else g + term
            if g is not None:
                y = y + g[r:r + tr, :]
        o_ref[0, pl.ds(r0, tr), :] = _layernorm_silu(y + bias, lnw, lnb).astype(o_ref.dtype)
        return c

    lax.fori_loop(0, l // tr, tile, 0)


def conv_prompt(u, conv_w, conv_b, ln_w, ln_b):
    b, l, c = u.shape
    vec = lambda i: (0, 0)
    return pl.pallas_call(
        _conv_kernel,
        grid=(b,),
        in_specs=[pl.BlockSpec((1, l, c), lambda i: (i, 0, 0)),
                  pl.BlockSpec((CONV_K, c), vec),
                  pl.BlockSpec((1, c), vec), pl.BlockSpec((1, c), vec), pl.BlockSpec((1, c), vec)],
        out_specs=pl.BlockSpec((1, l, c), lambda i: (i, 0, 0)),
        out_shape=jax.ShapeDtypeStruct((b, l, c), BF16),
        scratch_shapes=[pltpu.VMEM((CONV_PAD + l + SUBLANES, c), F32)],
        name="conv_prompt",
    )(u, conv_w, conv_b.reshape(1, c), ln_w.reshape(1, c), ln_b.reshape(1, c))


def _outproj_kernel(x_ref, oa_ref, oc_ref, w_ref, g_ref, o_ref):
    m = _mm(oa_ref[0], w_ref[:SB_WIDTH, :]) + _mm(oc_ref[0], w_ref[SB_WIDTH:, :])
    o_ref[0] = x_ref[0] + g_ref[0] * m


def outproj(x, oa, oc, w_out, gate, tm):
    b, l, d = x.shape
    row = lambda i, j: (i, j, 0)
    return pl.pallas_call(
        _outproj_kernel,
        grid=(b, l // tm),
        in_specs=[pl.BlockSpec((1, tm, d), row),
                  pl.BlockSpec((1, tm, SB_WIDTH), row),
                  pl.BlockSpec((1, tm, CONV_WIDTH), row),
                  _weight_spec(w_out),
                  _mod_spec(gate, tm)],
        out_specs=pl.BlockSpec((1, tm, d), row),
        out_shape=jax.ShapeDtypeStruct((b, l, d), F32),
        name="outproj",
    )(x, oa, oc, _weight_arr(w_out), gate)


def _normmod_kernel(x_ref, nw_ref, sh_ref, sc_ref, o_ref):
    o_ref[0] = _rms_mod(x_ref[0], nw_ref[...], sh_ref[0], sc_ref[0]).astype(o_ref.dtype)


def normmod(x, nw, shift, scale, tm, dtype):
    b, l, d = x.shape
    row = lambda i, j: (i, j, 0)
    return pl.pallas_call(
        _normmod_kernel,
        grid=(b, l // tm),
        in_specs=[pl.BlockSpec((1, tm, d), row), pl.BlockSpec((1, d), lambda i, j: (0, 0)),
                  _mod_spec(shift, tm), _mod_spec(scale, tm)],
        out_specs=pl.BlockSpec((1, tm, d), row),
        out_shape=jax.ShapeDtypeStruct((b, l, d), dtype),
        name="normmod",
    )(x, nw.reshape(1, d), shift, scale)


def _cmul(ar, ai, br, bi):
    return ar * br - ai * bi, ar * bi + ai * br


def s5_discretize(lam_re, lam_im, b_re, b_im, log_dt):
    dt = jnp.exp(log_dt)[:, None]
    mag = jnp.exp(lam_re * dt)
    ar, ai = mag * jnp.cos(lam_im * dt), mag * jnp.sin(lam_im * dt)
    den = lam_re * lam_re + lam_im * lam_im
    fr = ((ar - 1) * lam_re + ai * lam_im) / den
    fi = (ai * lam_re - (ar - 1) * lam_im) / den
    bbr, bbi = _cmul(fr[..., None], fi[..., None], b_re, b_im)
    return ar, ai, bbr, bbi


def s5_chunk_operators(ar, ai, bbr, bbi, c_re, c_im, d):
    g, p, c = bbr.shape
    ch = S5_CHUNK
    gl = LANES // c
    nt = g // gl
    pr, pi = [jnp.ones_like(ar)], [jnp.zeros_like(ai)]
    for _ in range(ch):
        nr, ni = _cmul(pr[-1], pi[-1], ar, ai)
        pr.append(nr)
        pi.append(ni)
    pr, pi = jnp.stack(pr), jnp.stack(pi)
    car, cai = _cmul(c_re[None], c_im[None], pr[:, :, None, :], pi[:, :, None, :])
    ein = functools.partial(jnp.einsum, precision=HIGHEST)
    ktau = ein('tgcp,gpd->tgcd', car[:ch], bbr) - ein('tgcp,gpd->tgcd', cai[:ch], bbi)
    ktau = ktau.at[0].add(jnp.eye(c, dtype=F32)[None] * d.reshape(g, c)[:, :, None])
    lane_grp = (jnp.arange(ch * LANES) // c) % gl
    st_grp = jnp.arange(gl * p) // p
    kt = ktau.reshape(ch, nt, gl, c, c).transpose(0, 1, 2, 4, 3).reshape(ch, nt, LANES, c)
    kblk = jnp.concatenate([kt] * gl, axis=-1) * (lane_grp[:LANES, None] == lane_grp[None, :LANES])
    s_idx = jnp.arange(ch)
    tau = s_idx[None, :] - s_idx[:, None]
    lag = (tau[None] == s_idx[:, None, None]).astype(F32)
    kbig = ein('xst,xjab->jsatb', lag, kblk).reshape(nt, ch * LANES, ch * LANES)

    def state_cols(w):
        w = w.reshape(ch, nt, gl, p, c).transpose(1, 0, 2, 4, 3).reshape(nt, ch * LANES, p)
        return jnp.concatenate([w] * gl, axis=-1) * (lane_grp[:, None] == st_grp[None, :])

    rev = ch - 1 - s_idx
    wr, wi = _cmul(pr[rev][..., None], pi[rev][..., None], bbr[None], bbi[None])
    bst = jnp.concatenate([state_cols(wr), state_cols(wi)], axis=-1)

    def state_rows(w):
        w = w.reshape(ch, nt, gl, c, p).transpose(1, 4, 0, 2, 3).reshape(nt, p, ch * LANES)
        w = jnp.broadcast_to(w[:, None], (nt, gl, p, ch * LANES)).reshape(nt, gl * p, ch * LANES)
        return w * (st_grp[:, None] == lane_grp[None, :])

    cb = jnp.concatenate([state_rows(car[1:]), -state_rows(cai[1:])], axis=1)
    pr8, pi8 = pr[ch].reshape(nt, gl * p), pi[ch].reshape(nt, gl * p)
    a_chunk = jnp.stack([jnp.concatenate([pr8, pr8], -1), jnp.concatenate([-pi8, pi8], -1)], axis=1)
    return kbig.astype(BF16), bst.astype(BF16), cb.astype(BF16), a_chunk


def _s5_kernel(u_ref, kbig_ref, bst_ref, cb_ref, a_ref, y_ref, hl_ref, s_scr, y_scr):
    ch = S5_CHUNK
    nk = u_ref.shape[1] // ch
    ucat = jnp.concatenate([u_ref[0, pl.ds(s, nk, stride=ch), :].astype(BF16) for s in range(ch)], axis=1)
    s_scr[...] = _dot(ucat, bst_ref[0])
    a_same = a_ref[0, 0:1, :]
    a_swap = a_ref[0, 1:2, :]
    half = a_same.shape[-1] // 2

    def rows(gi, h):
        base = pl.multiple_of(gi * SUBLANES, SUBLANES)
        blk = s_scr[pl.ds(base, SUBLANES), :]
        before = []
        for j in range(SUBLANES):
            before.append(h)
            h = h * a_same + pltpu.roll(h, half, 1) * a_swap + blk[j:j + 1, :]
        s_scr[pl.ds(base, SUBLANES), :] = jnp.concatenate(before, axis=0)
        return h

    hl_ref[0, 0] = lax.fori_loop(0, nk // SUBLANES, rows, jnp.zeros((1, 2 * half), F32))
    y = _dot(ucat, kbig_ref[0]) + _dot(s_scr[...].astype(BF16), cb_ref[0])
    for t in range(ch):
        y_scr[pl.ds(t, nk, stride=ch), :] = jax.nn.gelu(y[:, t * LANES:(t + 1) * LANES])
    y_ref[0] = y_scr[...].astype(y_ref.dtype)


def s5_prompt(u, kbig, bst, cb, a_chunk):
    b, l, dm = u.shape
    nt = kbig.shape[0]
    ch = S5_CHUNK
    sw = cb.shape[1]
    tile = lambda a: pl.BlockSpec((1,) + a.shape[1:], lambda j, i: (j, 0, 0))
    y, hl = pl.pallas_call(
        _s5_kernel,
        grid=(nt, b),
        in_specs=[pl.BlockSpec((1, l, LANES), lambda j, i: (i, 0, j)),
                  tile(kbig), tile(bst), tile(cb), tile(a_chunk)],
        out_specs=[pl.BlockSpec((1, l, LANES), lambda j, i: (i, 0, j)),
                   pl.BlockSpec((1, 1, 1, sw), lambda j, i: (i, j, 0, 0))],
        out_shape=[jax.ShapeDtypeStruct((b, l, dm), BF16), jax.ShapeDtypeStruct((b, nt, 1, sw), F32)],
        scratch_shapes=[pltpu.VMEM((l // ch, sw), F32), pltpu.VMEM((l, LANES), F32)],
        name="s5_prompt",
    )(u, kbig, bst, cb, a_chunk)
    hl = hl.reshape(b, nt, 2, -1, SSM_STATE)
    return y, hl[:, :, 0].reshape(b, -1, SSM_STATE), hl[:, :, 1].reshape(b, -1, SSM_STATE)


def _glu_kernel(x_ref, y_ref, w_ref, g_ref, o_ref):
    z = _mm(y_ref[0], w_ref[...])
    d = o_ref.shape[-1]
    o_ref[0] = x_ref[0] + g_ref[0] * (z[:, :d] * jax.nn.sigmoid(z[:, d:]))


def glu_residual(x, y, w_glu, gate, tm):
    b, l, d = x.shape
    row = lambda i, j: (i, j, 0)
    return pl.pallas_call(
        _glu_kernel,
        grid=(b, l // tm),
        in_specs=[pl.BlockSpec((1, tm, d), row), pl.BlockSpec((1, tm, d), row),
                  _weight_spec(w_glu),
                  _mod_spec(gate, tm)],
        out_specs=pl.BlockSpec((1, tm, d), row),
        out_shape=jax.ShapeDtypeStruct((b, l, d), F32),
        name="glu_residual",
    )(x, y, _weight_arr(w_glu), gate)


def _route(logits_t, b_col):
    e_tot = logits_t.shape[0]
    epg = EXPERTS_PER_GROUP
    ng = e_tot // epg
    scores = jax.nn.sigmoid(logits_t)
    sel_all = scores + b_col
    sel = [sel_all[e:e + 1, :] for e in range(e_tot)]
    sc = [scores[e:e + 1, :] for e in range(e_tot)]
    gsum = []
    for g in range(ng):
        a, b, c, d = sel[epg * g:epg * g + epg]
        hi1, lo1, hi2, lo2 = jnp.maximum(a, b), jnp.minimum(a, b), jnp.maximum(c, d), jnp.minimum(c, d)
        gsum.append(jnp.maximum(hi1, hi2) + jnp.maximum(jnp.minimum(hi1, hi2), jnp.maximum(lo1, lo2)))
    best = gsum[0]
    gi = jnp.zeros_like(best, dtype=jnp.int32)
    for g in range(1, ng):
        upd = gsum[g] > best
        gi = jnp.where(upd, g, gi)
        best = jnp.where(upd, gsum[g], best)
    picked = []
    for e in range(e_tot):
        g, j = divmod(e, epg)
        ahead = jnp.zeros_like(gi)
        for j2 in range(epg):
            if j2 == j:
                continue
            o = sel[epg * g + j2]
            before = (o >= sel[e]) if j2 < j else (o > sel[e])
            ahead = ahead + before.astype(jnp.int32)
        picked.append((gi == g) & (ahead < 2))
    wk = [jnp.where(picked[e], sc[e], 0.0) for e in range(e_tot)]
    tot = wk[0]
    for e in range(1, e_tot):
        tot = tot + wk[e]
    return jnp.concatenate([w / tot for w in wk], axis=0)


MOE_SUB = 512


def _router_kernel(x_ref, nw_ref, sh_ref, sc_ref, wr_ref, br_ref, h_ref, g_ref, rank_ref, cnt_ref):
    h = _rms_mod(x_ref[0], nw_ref[...], sh_ref[0], sc_ref[0])
    h_ref[0] = h.astype(h_ref.dtype)
    logits_t = lax.dot_general(wr_ref[...], h, NT_DIMS, preferred_element_type=F32, precision=HIGHEST)
    gates = _route(logits_t, br_ref[...])
    e_tot, l = gates.shape
    sub = min(MOE_SUB, l)
    picked = gates > 0.0
    within = lax.broadcasted_iota(jnp.int32, (e_tot, l), 1) % sub
    c = picked.astype(F32)
    d = 1
    while d < sub:
        c = c + jnp.where(within >= d, pltpu.roll(c, d, 1), 0.0)
        d *= 2
    rank = jnp.where(picked, c - 1.0, -1.0).astype(jnp.int32)
    for e in range(e_tot):
        g_ref[0, e] = gates[e:e + 1, :]
        rank_ref[0, e] = rank[e:e + 1, :]
    cnt_ref[0] = jnp.concatenate([c[:, s * sub + sub - 1:s * sub + sub] for s in range(l // sub)],
                                 axis=1).astype(jnp.int32)


def router(x, nw, shift, scale, w_router, b_router, h_dtype):
    b, l, d = x.shape
    e = w_router.shape[1]
    nsub = l // min(MOE_SUB, l)
    row = lambda i: (i, 0, 0)
    const = lambda i: (0, 0)
    mod = lambda a: pl.BlockSpec((1,) + a.shape[1:], row)
    return pl.pallas_call(
        _router_kernel,
        grid=(b,),
        in_specs=[pl.BlockSpec((1, l, d), row), pl.BlockSpec((1, d), const), mod(shift), mod(scale),
                  pl.BlockSpec((e, d), const), pl.BlockSpec((e, 1), const)],
        out_specs=[pl.BlockSpec((1, l, d), row),
                   pl.BlockSpec((1, e, 1, l), lambda i: (i, 0, 0, 0)),
                   pl.BlockSpec((1, e, 1, l), lambda i: (i, 0, 0, 0)),
                   pl.BlockSpec((1, e, nsub), row)],
        out_shape=[jax.ShapeDtypeStruct((b, l, d), h_dtype), jax.ShapeDtypeStruct((b, e, 1, l), F32),
                   jax.ShapeDtypeStruct((b, e, 1, l), jnp.int32), jax.ShapeDtypeStruct((b, e, nsub), jnp.int32)],
        name="router",
    )(x, nw.reshape(1, d), shift, scale, w_router.T, b_router.reshape(e, 1))


def _moe_dense_kernel(h_ref, gt_ref, wg_ref, wu_ref, wd_ref, x_ref, g6_ref, o_ref, acc_ref):
    e = pl.program_id(2)

    @pl.when(e == 0)
    def _():
        acc_ref[...] = jnp.zeros_like(acc_ref)

    h = h_ref[0]
    hg = _mm(h, wg_ref[0])
    hu = _mm(h, wu_ref[0])
    act = _silu(hg) * hu * gt_ref[0, 0]
    acc_ref[...] += _mm(act, wd_ref[0])

    @pl.when(e == pl.num_programs(2) - 1)
    def _():
        o_ref[0] = x_ref[0] + g6_ref[0] * acc_ref[...]


def moe_dense(h, gates_col, w_gate, w_up, w_down, layer, x, gate6, tm):
    b, l, d = x.shape
    _, e, _, f = w_gate.shape
    row = lambda i, j, k: (i, j, 0)
    wspec = lambda s: pl.BlockSpec((None, 1) + s, lambda i, j, k: (layer, k, 0, 0))
    return pl.pallas_call(
        _moe_dense_kernel,
        grid=(b, l // tm, e),
        in_specs=[pl.BlockSpec((1, tm, d), row),
                  pl.BlockSpec((1, 1, tm, 1), lambda i, j, k: (i, k, j, 0)),
                  wspec((d, f)), wspec((d, f)), wspec((f, d)),
                  pl.BlockSpec((1, tm, d), row),
                  _mod_spec(gate6, tm)],
        out_specs=pl.BlockSpec((1, tm, d), row),
        out_shape=jax.ShapeDtypeStruct((b, l, d), F32),
        scratch_shapes=[pltpu.VMEM((tm, d), F32)],
        name="moe_dense",
    )(h, gates_col, w_gate, w_up, w_down, x, gate6)


MOE_CAP = 128
MOE_TM = 2048
TN_DIMS = (((0,), (0,)), ((), ()))


def _moe_sparse_kernel(cnt_ref, h_ref, rank_ref, gt_ref, wg_ref, wu_ref, wd_ref, x_ref, g6_ref, o_ref):
    bi, j, e = pl.program_id(0), pl.program_id(1), pl.program_id(2)
    tm = o_ref.shape[1]
    sub = min(MOE_SUB, tm)
    nsub = tm // sub
    cap = MOE_CAP

    @pl.when(e == 0)
    def _():
        o_ref[...] = jnp.zeros_like(o_ref)

    row_io = lax.broadcasted_iota(jnp.int32, (cap, sub), 0)

    def chunks(subs, c):
        ps, xs, gs = [], [], []
        for s in subs:
            cols = slice(s * sub, (s + 1) * sub)
            pick = rank_ref[0, 0, :, cols] == row_io + c * cap
            p = pick.astype(BF16)
            ps.append(p)
            xs.append(_dot(p, h_ref[0, cols, :]).astype(BF16))
            gs.append(jnp.sum(jnp.where(pick, gt_ref[0, 0, :, cols], 0.0), axis=1, keepdims=True))
        xg = jnp.concatenate(xs, axis=0)
        act = _silu(_dot(xg, wg_ref[0])) * _dot(xg, wu_ref[0]) * jnp.concatenate(gs, axis=0)
        y = _dot(act.astype(BF16), wd_ref[0]).astype(BF16)
        for n, s in enumerate(subs):
            o_ref[0, s * sub:(s + 1) * sub, :] += lax.dot_general(ps[n], y[n * cap:(n + 1) * cap, :],
                                                                 TN_DIMS, preferred_element_type=F32)

    chunks(list(range(nsub)), 0)
    for s in range(nsub):
        n = cnt_ref[((bi * pl.num_programs(1) + j) * nsub + s) * pl.num_programs(2) + e]

        def more(c, carry, s=s):
            chunks([s], c)
            return carry

        lax.fori_loop(1, (n + cap - 1) // cap, more, 0)

    @pl.when(e == pl.num_programs(2) - 1)
    def _():
        o_ref[0] = x_ref[0] + g6_ref[0] * o_ref[0]


def moe_sparse(h, gates, rank, cnt, w_gate, w_up, w_down, layer, x, gate6):
    b, l, d = x.shape
    _, e, _, f = w_gate.shape
    tm = min(MOE_TM, l)
    row = lambda i, j, k, c: (i, j, 0)
    per_e = lambda i, j, k, c: (i, k, 0, j)
    wsp = lambda s: pl.BlockSpec((None, 1) + s, lambda i, j, k, c: (layer, k, 0, 0))
    grid_spec = pltpu.PrefetchScalarGridSpec(
        num_scalar_prefetch=1,
        grid=(b, l // tm, e),
        in_specs=[pl.BlockSpec((1, tm, d), row),
                  pl.BlockSpec((1, 1, 1, tm), per_e), pl.BlockSpec((1, 1, 1, tm), per_e),
                  wsp((d, f)), wsp((d, f)), wsp((f, d)),
                  pl.BlockSpec((1, tm, d), row),
                  _mod_spec(gate6, tm)],
        out_specs=pl.BlockSpec((1, tm, d), row),
    )
    cnt_flat = cnt.transpose(0, 2, 1).reshape(-1)
    return pl.pallas_call(
        _moe_sparse_kernel,
        grid_spec=grid_spec,
        out_shape=jax.ShapeDtypeStruct((b, l, d), F32),
        name="moe_sparse",
    )(cnt_flat, h, rank, gates, w_gate, w_up, w_down, x, gate6)


def _final_kernel(x_ref, nw_ref, o_ref):
    x = x_ref[0]
    o_ref[0] = x * lax.rsqrt(jnp.mean(x * x, axis=-1, keepdims=True) + EPS) * nw_ref[...]


def final_norm(x, nw, tm):
    b, l, d = x.shape
    row = lambda i, j: (i, j, 0)
    return pl.pallas_call(
        _final_kernel,
        grid=(b, l // tm),
        in_specs=[pl.BlockSpec((1, tm, d), row), pl.BlockSpec((1, d), lambda i, j: (0, 0))],
        out_specs=pl.BlockSpec((1, tm, d), row),
        out_shape=jax.ShapeDtypeStruct((b, l, d), F32),
        name="final_norm",
    )(x, nw.reshape(1, d))


PAGES_PER_STEP = 8


def _sb_attn_sample_kernel(pt_ref, bias_ref, q_ref, *rest):
    npp = PAGES_PER_STEP
    k_refs, v_refs = rest[:npp], rest[npp:2 * npp]
    o_ref, acc_ref, car_ref = rest[2 * npp:]
    s = pl.program_id(1)
    nh = SB_HEADS

    @pl.when(s == 0)
    def _():
        acc_ref[...] = jnp.zeros_like(acc_ref)
        car_ref[...] = jnp.zeros_like(car_ref)

    pg = k_refs[0].shape[-1]
    qb = jnp.broadcast_to(q_ref[0], (nh, SB_HEAD_DIM, pg))
    bias_col = bias_ref[...]
    n = npp * pg
    lane = lax.broadcasted_iota(jnp.int32, (nh, n), 1)
    order = list(range(npp - 1, -1, -1))
    z = jnp.concatenate([jnp.sum(k_refs[r][...] * qb, axis=1) for r in order], axis=1) + bias_col
    sp = _softplus(z)
    l1m = -sp
    c = l1m
    d = 1
    while d < n:
        c = c + jnp.where(lane + d < n, pltpu.roll(c, n - d, 1), 0.0)
        d *= 2
    w = jnp.exp(z - sp + car_ref[:, 0:1] + c - l1m)
    car_ref[...] += jnp.broadcast_to(c[:, 0:1], car_ref.shape)
    acc = acc_ref[...]
    for i, r in enumerate(order):
        acc = acc + v_refs[r][...] * w[:, None, i * pg:(i + 1) * pg]
    acc_ref[...] = acc

    @pl.when(s == pl.num_programs(1) - 1)
    def _():
        o_ref[0] = jnp.sum(acc_ref[...], axis=-1)


def sb_attn_sample(q, cache_kt, cache_vt, layer, page_table, bias):
    n, nh, dh = q.shape
    n_pages = page_table.shape[1]
    pg = cache_kt.shape[-1]
    npp = PAGES_PER_STEP
    nsteps = n_pages // npp

    def page_spec(r):
        def imap(b, s, pt):
            return (layer, pt[b * n_pages + (n_pages - 1 - (s * npp + r))], 0, 0, 0)
        return pl.BlockSpec((None, None, nh, dh, pg), imap)

    grid_spec = pltpu.PrefetchScalarGridSpec(
        num_scalar_prefetch=1,
        grid=(n, nsteps),
        in_specs=([pl.BlockSpec((nh, 1), lambda b, s, pt: (0, 0)),
                   pl.BlockSpec((1, nh, dh, 1), lambda b, s, pt: (b, 0, 0, 0))]
                  + [page_spec(r) for r in range(npp)] * 2),
        out_specs=pl.BlockSpec((1, nh, dh), lambda b, s, pt: (b, 0, 0)),
        scratch_shapes=[pltpu.VMEM((nh, dh, pg), F32), pltpu.VMEM((nh, pg), F32)],
    )
    return pl.pallas_call(
        _sb_attn_sample_kernel,
        grid_spec=grid_spec,
        out_shape=jax.ShapeDtypeStruct((n, nh, dh), F32),
        name="sb_attn_sample",
    )(page_table.reshape(-1), bias.reshape(nh, 1), q.reshape(n, nh, dh, 1),
      *([cache_kt] * npp), *([cache_vt] * npp))


def _conv_sample_kernel(u_ref, st_ref, w_ref, b_ref, lnw_ref, lnb_ref, o_ref):
    km1 = CONV_K - 1
    w = w_ref[...]
    y = jnp.sum(st_ref[...] * w[None, :km1, :], axis=1) + u_ref[...] * w[km1:, :] + b_ref[...]
    o_ref[...] = _layernorm_silu(y, lnw_ref[...], lnb_ref[...])


def conv_sample(u, state, conv_w, conv_b, ln_w, ln_b):
    n, c = u.shape
    return pl.pallas_call(
        _conv_sample_kernel,
        out_shape=jax.ShapeDtypeStruct((n, c), F32),
        name="conv_sample",
    )(u, state, conv_w, conv_b.reshape(1, c), ln_w.reshape(1, c), ln_b.reshape(1, c))


def _s5_sample_kernel(u_ref, h0r_ref, h0i_ref, ar_ref, ai_ref, bbr_ref, bbi_ref, cr_ref, ci_ref, d_ref,
                      y_ref, hr_ref, hi_ref):
    u = u_ref[...]
    ein = functools.partial(jnp.einsum, precision=HIGHEST, preferred_element_type=F32)
    hr0, hi0 = h0r_ref[...], h0i_ref[...]
    ar, ai = ar_ref[...], ai_ref[...]
    hr = ar * hr0 - ai * hi0 + ein('gnc,gcp->gnp', u, bbr_ref[...])
    hi = ar * hi0 + ai * hr0 + ein('gnc,gcp->gnp', u, bbi_ref[...])
    hr_ref[...] = hr
    hi_ref[...] = hi
    y = ein('gnp,gpc->gnc', hr, cr_ref[...]) - ein('gnp,gpc->gnc', hi, ci_ref[...]) + d_ref[...] * u
    y_ref[...] = jax.nn.gelu(y)


def s5_sample(u, h0_re, h0_im, ar, ai, bbr, bbi, c_re, c_im, d):
    n, dm = u.shape
    g, p, c = bbr.shape
    t = lambda a: a.transpose(1, 0, 2)
    sh = jax.ShapeDtypeStruct
    y, hr, hi = pl.pallas_call(
        _s5_sample_kernel,
        out_shape=[sh((g, n, c), F32), sh((g, n, p), F32), sh((g, n, p), F32)],
        name="s5_sample",
    )(t(u.reshape(n, g, c)), t(h0_re), t(h0_im), ar[:, None, :], ai[:, None, :],
      bbr.transpose(0, 2, 1), bbi.transpose(0, 2, 1), c_re.transpose(0, 2, 1), c_im.transpose(0, 2, 1),
      d.reshape(g, 1, c))
    return t(y).reshape(n, dm), t(hr), t(hi)


def kernel(x_prompt, x_sample, c_prompt, c_sample, cache_k, cache_v, page_table, state_conv, state_ssm_re, state_ssm_im, w_ada, b_ada, norm_w, final_norm_w, w_in_mix, w_out_mix, sb_bias, conv_w, conv_b, conv_ln_w, conv_ln_b, ssm_lambda_re, ssm_lambda_im, ssm_b_re, ssm_b_im, ssm_c_re, ssm_c_im, ssm_d, ssm_log_dt, w_glu, w_router, b_router, w_gate, w_up, w_down):
    nb, l, d = x_prompt.shape
    ns = x_sample.shape[0]
    depth = w_ada.shape[0]
    n_attn = cache_k.shape[0]
    assert nb == SUBLANES and x_sample.shape[1] == 1
    tm = 512

    ada = ada_all(jnp.concatenate([c_prompt, c_sample], axis=0), w_ada, b_ada)
    ada = ada.reshape(depth, nb + ns, N_ADA, d)
    cache_kt = cache_k.transpose(0, 1, 3, 4, 2)
    cache_vt = cache_v.transpose(0, 1, 3, 4, 2)

    w_gate_b, w_up_b, w_down_b = w_gate.astype(BF16), w_up.astype(BF16), w_down.astype(BF16)

    xp = x_prompt
    xs = x_sample.reshape(1, ns, d)
    kp_l, vp_l, ks_l, vs_l, cvp_l, cvs_l = [], [], [], [], [], []
    srp_l, sip_l, srs_l, sis_l = [], [], [], []
    for layer in range(depth):
        ap = [ada[layer, :nb, j].reshape(nb, 1, d) for j in range(N_ADA)]
        asm = [ada[layer, nb:, j].reshape(1, ns, d) for j in range(N_ADA)]
        nw = norm_w[layer]
        i = layer // 2
        if layer % 2 == 0:
            w_in_b, w_out_b = w_in_mix[i].astype(BF16), w_out_mix[i].astype(BF16)
            q, k, v, u = inproj(xp, nw[0], ap[0], ap[1], w_in_b, tm, BF16, Q_SCALE * LOG2E)
            oa = sb_attn_prompt(q, k, v, sb_bias[i])
            oc = conv_prompt(u, conv_w[i], conv_b[i], conv_ln_w[i], conv_ln_b[i])
            xp = outproj(xp, oa, oc, w_out_b, ap[2], tm)
            kp_l.append(k.reshape(nb, l, SB_HEADS, SB_HEAD_DIM))
            vp_l.append(v.reshape(nb, l, SB_HEADS, SB_HEAD_DIM))
            cvp_l.append(u[:, l - (CONV_K - 1):, :])

            qs, ks_, vs_, us = inproj(xs, nw[0], asm[0], asm[1], (w_in_mix, i), ns, F32, Q_SCALE)
            oas = sb_attn_sample(qs.reshape(ns, SB_HEADS, SB_HEAD_DIM), cache_kt, cache_vt, i, page_table, sb_bias[i])
            ocs = conv_sample(us[0], state_conv[i], conv_w[i], conv_b[i], conv_ln_w[i], conv_ln_b[i])
            xs = outproj(xs, oas.reshape(1, ns, SB_WIDTH), ocs[None], (w_out_mix, i), asm[2], ns)
            ks_l.append(ks_.reshape(ns, 1, SB_HEADS, SB_HEAD_DIM))
            vs_l.append(vs_.reshape(ns, 1, SB_HEADS, SB_HEAD_DIM))
            cvs_l.append(jnp.concatenate([state_conv[i][:, 1:, :], us.reshape(ns, 1, CONV_WIDTH)], axis=1))
        else:
            ar, ai, bbr, bbi = s5_discretize(ssm_lambda_re[i], ssm_lambda_im[i], ssm_b_re[i], ssm_b_im[i],
                                             ssm_log_dt[i])
            s5_ops = s5_chunk_operators(ar, ai, bbr, bbi, ssm_c_re[i], ssm_c_im[i], ssm_d[i])
            hp = normmod(xp, nw[0], ap[0], ap[1], tm, F32)
            yp, srp, sip = s5_prompt(hp, *s5_ops)
            xp = glu_residual(xp, yp, w_glu[i].astype(BF16), ap[2], tm)
            srp_l.append(srp)
            sip_l.append(sip)

            hs = normmod(xs, nw[0], asm[0], asm[1], ns, F32)
            ys, srs, sis = s5_sample(hs[0], state_ssm_re[i], state_ssm_im[i], ar, ai, bbr, bbi,
                                     ssm_c_re[i], ssm_c_im[i], ssm_d[i])
            xs = glu_residual(xs, ys[None], (w_glu, i), asm[2], ns)
            srs_l.append(srs)
            sis_l.append(sis)
        hp, gates, rank, cnt = router(xp, nw[1], ap[3], ap[4], w_router, b_router, BF16)
        xp = moe_sparse(hp, gates, rank, cnt, w_gate_b, w_up_b, w_down_b, layer, xp, ap[5])
        hs, gates, _, _ = router(xs, nw[1], asm[3], asm[4], w_router, b_router, F32)
        xs = moe_dense(hs, gates.reshape(1, -1, ns, 1), w_gate, w_up, w_down, layer, xs, asm[5], ns)
    y_prompt = final_norm(xp, final_norm_w, tm)
    y_sample = final_norm(xs, final_norm_w, ns).reshape(ns, 1, d)
    return (y_prompt, y_sample,
            jnp.stack(kp_l), jnp.stack(vp_l), jnp.stack(ks_l), jnp.stack(vs_l),
            jnp.stack(cvp_l), jnp.stack(cvs_l),
            jnp.stack(srp_l), jnp.stack(sip_l), jnp.stack(srs_l), jnp.stack(sis_l))
```

```python
import functools
import math

import jax
import jax.numpy as jnp
from jax import lax
from jax.experimental import pallas as pl
from jax.experimental.pallas import tpu as pltpu

F32 = jnp.float32
BF16 = jnp.bfloat16
HIGHEST = lax.Precision.HIGHEST

EPS = 1e-6
LN_EPS = 1e-5
SB_HEADS = 8
SB_HEAD_DIM = 64
SB_WIDTH = SB_HEADS * SB_HEAD_DIM
CONV_WIDTH = 512
CONV_K = 31
SSM_GROUP = 16
SSM_STATE = 64
N_EXPERTS = 16
EXPERTS_PER_GROUP = 4
N_ADA = 6

LANES = 128
SUBLANES = 8
S5_CHUNK = SUBLANES
NT_DIMS = (((1,), (1,)), ((), ()))


def _dot(a, b):
    return jnp.dot(a, b, preferred_element_type=F32)


def _dotf(a, b):
    return jnp.dot(a, b, preferred_element_type=F32, precision=HIGHEST)


def _mm(a, w):
    if w.dtype == BF16:
        return _dot(a.astype(BF16), w)
    return _dotf(a, w)


def _silu(x):
    return x * jax.nn.sigmoid(x)


def _rms_mod(x, nw, shift, scale):
    y = x * lax.rsqrt(jnp.mean(x * x, axis=-1, keepdims=True) + EPS) * nw
    return y * (1 + scale) + shift


def _mod_spec(a, tm):
    d = a.shape[-1]
    if a.shape[1] == 1:
        return pl.BlockSpec((1, 1, d), lambda i, j, *_: (i, 0, 0))
    return pl.BlockSpec((1, tm, d), lambda i, j, *_: (i, j, 0))


def _weight_spec(w):
    if isinstance(w, tuple):
        stack, layer = w
        return pl.BlockSpec((None,) + stack.shape[1:], lambda i, j: (layer, 0, 0))
    return pl.BlockSpec(w.shape, lambda i, j: (0, 0))


def _weight_arr(w):
    return w[0] if isinstance(w, tuple) else w


def _split_bf16(x):
    hi = x.astype(BF16)
    lo = (x - hi.astype(F32)).astype(BF16)
    return hi, lo


def _ada_kernel(c_ref, w_ref, b_ref, o_ref):
    o_ref[0] = _dotf(_silu(c_ref[...]), w_ref[0]) + b_ref[0]


def ada_all(c, w_ada, b_ada):
    depth, d, n = w_ada.shape
    r = c.shape[0]
    tn = 1024
    return pl.pallas_call(
        _ada_kernel,
        grid=(depth, n // tn),
        in_specs=[pl.BlockSpec((r, d), lambda l, j: (0, 0)),
                  pl.BlockSpec((1, d, tn), lambda l, j: (l, 0, j)),
                  pl.BlockSpec((1, 1, tn), lambda l, j: (l, 0, j))],
        out_specs=pl.BlockSpec((1, r, tn), lambda l, j: (l, 0, j)),
        out_shape=jax.ShapeDtypeStruct((depth, r, n), F32),
        name="ada",
    )(c, w_ada, b_ada.reshape(depth, 1, n))


LOG2E = math.log2(math.e)
Q_SCALE = 1.0 / math.sqrt(SB_HEAD_DIM)


def _inproj_kernel(q_scale, x_ref, nw_ref, sh_ref, sc_ref, w_ref, q_ref, k_ref, v_ref, u_ref):
    h = _rms_mod(x_ref[0], nw_ref[...], sh_ref[0], sc_ref[0])
    p = _mm(h, w_ref[...])
    w = SB_WIDTH
    q_ref[0] = (p[:, :w] * q_scale).astype(q_ref.dtype)
    k_ref[0] = p[:, w:2 * w]
    v_ref[0] = p[:, 2 * w:3 * w]
    a = p[:, 3 * w:3 * w + CONV_WIDTH]
    g = p[:, 3 * w + CONV_WIDTH:]
    u_ref[0] = a * jax.nn.sigmoid(g)


def inproj(x, nw, shift, scale, w_in, tm, q_dtype, q_scale):
    b, l, d = x.shape
    row = lambda i, j: (i, j, 0)
    o = jax.ShapeDtypeStruct((b, l, SB_WIDTH), F32)
    return pl.pallas_call(
        functools.partial(_inproj_kernel, q_scale),
        grid=(b, l // tm),
        in_specs=[pl.BlockSpec((1, tm, d), row),
                  pl.BlockSpec((1, d), lambda i, j: (0, 0)),
                  _mod_spec(shift, tm), _mod_spec(scale, tm),
                  _weight_spec(w_in)],
        out_specs=[pl.BlockSpec((1, tm, SB_WIDTH), row)] * 4,
        out_shape=[jax.ShapeDtypeStruct((b, l, SB_WIDTH), q_dtype), o, o, o],
        name="inproj",
    )(x, nw.reshape(1, d), shift, scale, _weight_arr(w_in))


ATT_T = 512
ATT_SUB = LANES


def _softplus(z):
    return jnp.maximum(z, 0.0) + jnp.log(1.0 + jnp.exp(-jnp.abs(z)))


def _softplus2(z2):
    return jnp.maximum(z2, 0.0) + jnp.log2(1.0 + jnp.exp2(-jnp.abs(z2)))


def _sb_attn_kernel(bias_ref, q_ref, k_ref, v_ref, o_ref, acc_ref, run_ref):
    t, sub = ATT_T, ATT_SUB
    hp = pl.program_id(1)
    i = pl.program_id(2)
    q2 = q_ref[0]
    lane = lax.broadcasted_iota(jnp.int32, (t, LANES), 1)
    head_of_lane = lane // SB_HEAD_DIM
    qm = [jnp.where(head_of_lane == hh, q2, jnp.zeros_like(q2)) for hh in range(2)]
    bias = [bias_ref[2 * hp + hh] * LOG2E for hh in range(2)]
    r_io = lax.broadcasted_iota(jnp.int32, (sub, 2 * sub), 0)
    c_io = lax.broadcasted_iota(jnp.int32, (sub, 2 * sub), 1)
    tri_ones = ((r_io > c_io) | (c_io >= sub)).astype(BF16)
    acc_ref[...] = jnp.zeros_like(acc_ref)
    run_ref[...] = jnp.zeros_like(run_ref)
    causal = lax.broadcasted_iota(jnp.int32, (t, t), 1) < lax.broadcasted_iota(jnp.int32, (t, t), 0)

    def chunk(j, masked):
        start = pl.multiple_of(j * t, t)
        kb = k_ref[0, pl.ds(start, t), :].astype(BF16)
        vb = v_ref[0, pl.ds(start, t), :].astype(BF16)
        for hh in range(2):
            z = lax.dot_general(qm[hh], kb, NT_DIMS, preferred_element_type=F32) + bias[hh]
            run = run_ref[hh]
            ws = []
            for blk in range(t // sub - 1, -1, -1):
                cols = slice(blk * sub, (blk + 1) * sub)
                r0 = blk * sub if masked else 0
                zb = z[r0:, cols]
                sp = _softplus2(zb)
                if masked:
                    sp = jnp.where(causal[r0:, cols], sp, 0.0)
                sums = _dot(sp.astype(BF16), tri_ones)
                w = jnp.exp2(zb - sp - (run[r0:] + sums[:, :sub]))
                if masked:
                    w = jnp.where(causal[r0:, cols], w, 0.0)
                w = w.astype(BF16)
                later = run[r0:] + sums[:, sub:]
                if r0:
                    w = jnp.concatenate([jnp.zeros((r0, sub), BF16), w], axis=0)
                    later = jnp.concatenate([run[:r0], later], axis=0)
                ws.append(w)
                run = later
            run_ref[hh] = run
            acc_ref[hh] += _dot(jnp.concatenate(ws[::-1], axis=1), vb)

    chunk(i, True)

    def body(n, c):
        chunk(i - 1 - n, False)
        return c

    lax.fori_loop(0, i, body, 0)
    o_ref[0] = jnp.where(head_of_lane == 0, acc_ref[0], acc_ref[1]).astype(o_ref.dtype)


def sb_attn_prompt(q, k, v, bias):
    b, l, w = q.shape
    t = ATT_T
    return pl.pallas_call(
        _sb_attn_kernel,
        grid=(b, w // LANES, l // t),
        in_specs=[pl.BlockSpec(memory_space=pltpu.SMEM),
                  pl.BlockSpec((1, t, LANES), lambda bi, hp, i: (bi, i, hp)),
                  pl.BlockSpec((1, l, LANES), lambda bi, hp, i: (bi, 0, hp)),
                  pl.BlockSpec((1, l, LANES), lambda bi, hp, i: (bi, 0, hp))],
        out_specs=pl.BlockSpec((1, t, LANES), lambda bi, hp, i: (bi, i, hp)),
        out_shape=jax.ShapeDtypeStruct((b, l, w), BF16),
        scratch_shapes=[pltpu.VMEM((2, t, LANES), F32), pltpu.VMEM((2, t, ATT_SUB), F32)],
        name="sb_attn_prompt",
    )(bias, q, k, v)


CONV_TR = 32
CONV_PAD = 32


def _layernorm_silu(y, lnw, lnb):
    mu = jnp.mean(y, axis=-1, keepdims=True)
    yc = y - mu
    var = jnp.mean(yc * yc, axis=-1, keepdims=True)
    return _silu(yc * lax.rsqrt(var + LN_EPS) * lnw + lnb)


def _conv_kernel(u_ref, w_ref, b_ref, lnw_ref, lnb_ref, o_ref, full_ref):
    l = u_ref.shape[1]
    tr = CONV_TR
    full_ref[0:CONV_PAD, :] = jnp.zeros((CONV_PAD, CONV_WIDTH), F32)
    full_ref[CONV_PAD:CONV_PAD + l, :] = u_ref[0]
    full_ref[CONV_PAD + l:, :] = jnp.zeros((SUBLANES, CONV_WIDTH), F32)
    first = CONV_PAD - (CONV_K - 1)
    bias = b_ref[...]
    lnw = lnw_ref[...]
    lnb = lnb_ref[...]

    def tile(t, c):
        r0 = pl.multiple_of(t * tr, tr)
        y = jnp.zeros((tr, CONV_WIDTH), F32)
        for r in range(SUBLANES):
            g = None
            for m in range(first, first + CONV_K):
                if m % SUBLANES != r:
                    continue
                win = full_ref[pl.ds(r0 + (m - r), tr + SUBLANES), :]
                term = w_ref[m - first:m - first + 1, :] * win
                g = term if g is None else g + term
            if g is not None:
                y = y + g[r:r + tr, :]
        o_ref[0, pl.ds(r0, tr), :] = _layernorm_silu(y + bias, lnw, lnb).astype(o_ref.dtype)
        return c

    lax.fori_loop(0, l // tr, tile, 0)


def conv_prompt(u, conv_w, conv_b, ln_w, ln_b):
    b, l, c = u.shape
    vec = lambda i: (0, 0)
    return pl.pallas_call(
        _conv_kernel,
        grid=(b,),
        in_specs=[pl.BlockSpec((1, l, c), lambda i: (i, 0, 0)),
                  pl.BlockSpec((CONV_K, c), vec),
                  pl.BlockSpec((1, c), vec), pl.BlockSpec((1, c), vec), pl.BlockSpec((1, c), vec)],
        out_specs=pl.BlockSpec((1, l, c), lambda i: (i, 0, 0)),
        out_shape=jax.ShapeDtypeStruct((b, l, c), BF16),
        scratch_shapes=[pltpu.VMEM((CONV_PAD + l + SUBLANES, c), F32)],
        name="conv_prompt",
    )(u, conv_w, conv_b.reshape(1, c), ln_w.reshape(1, c), ln_b.reshape(1, c))


def _outproj_kernel(x_ref, oa_ref, oc_ref, w_ref, g_ref, o_ref):
    m = _mm(oa_ref[0], w_ref[:SB_WIDTH, :]) + _mm(oc_ref[0], w_ref[SB_WIDTH:, :])
    o_ref[0] = x_ref[0] + g_ref[0] * m


def outproj(x, oa, oc, w_out, gate, tm):
    b, l, d = x.shape
    row = lambda i, j: (i, j, 0)
    return pl.pallas_call(
        _outproj_kernel,
        grid=(b, l // tm),
        in_specs=[pl.BlockSpec((1, tm, d), row),
                  pl.BlockSpec((1, tm, SB_WIDTH), row),
                  pl.BlockSpec((1, tm, CONV_WIDTH), row),
                  _weight_spec(w_out),
                  _mod_spec(gate, tm)],
        out_specs=pl.BlockSpec((1, tm, d), row),
        out_shape=jax.ShapeDtypeStruct((b, l, d), F32),
        name="outproj",
    )(x, oa, oc, _weight_arr(w_out), gate)


def _normmod_kernel(x_ref, nw_ref, sh_ref, sc_ref, o_ref):
    o_ref[0] = _rms_mod(x_ref[0], nw_ref[...], sh_ref[0], sc_ref[0]).astype(o_ref.dtype)


def normmod(x, nw, shift, scale, tm, dtype):
    b, l, d = x.shape
    row = lambda i, j: (i, j, 0)
    return pl.pallas_call(
        _normmod_kernel,
        grid=(b, l // tm),
        in_specs=[pl.BlockSpec((1, tm, d), row), pl.BlockSpec((1, d), lambda i, j: (0, 0)),
                  _mod_spec(shift, tm), _mod_spec(scale, tm)],
        out_specs=pl.BlockSpec((1, tm, d), row),
        out_shape=jax.ShapeDtypeStruct((b, l, d), dtype),
        name="normmod",
    )(x, nw.reshape(1, d), shift, scale)


def _cmul(ar, ai, br, bi):
    return ar * br - ai * bi, ar * bi + ai * br


def s5_discretize(lam_re, lam_im, b_re, b_im, log_dt):
    dt = jnp.exp(log_dt)[:, None]
    mag = jnp.exp(lam_re * dt)
    ar, ai = mag * jnp.cos(lam_im * dt), mag * jnp.sin(lam_im * dt)
    den = lam_re * lam_re + lam_im * lam_im
    fr = ((ar - 1) * lam_re + ai * lam_im) / den
    fi = (ai * lam_re - (ar - 1) * lam_im) / den
    bbr, bbi = _cmul(fr[..., None], fi[..., None], b_re, b_im)
    return ar, ai, bbr, bbi


def s5_chunk_operators(ar, ai, bbr, bbi, c_re, c_im, d):
    g, p, c = bbr.shape
    ch = S5_CHUNK
    gl = LANES // c
    nt = g // gl
    pr, pi = [jnp.ones_like(ar)], [jnp.zeros_like(ai)]
    for _ in range(ch):
        nr, ni = _cmul(pr[-1], pi[-1], ar, ai)
        pr.append(nr)
        pi.append(ni)
    pr, pi = jnp.stack(pr), jnp.stack(pi)
    car, cai = _cmul(c_re[None], c_im[None], pr[:, :, None, :], pi[:, :, None, :])
    ein = functools.partial(jnp.einsum, precision=HIGHEST)
    ktau = ein('tgcp,gpd->tgcd', car[:ch], bbr) - ein('tgcp,gpd->tgcd', cai[:ch], bbi)
    ktau = ktau.at[0].add(jnp.eye(c, dtype=F32)[None] * d.reshape(g, c)[:, :, None])
    kt = ktau.reshape(ch, nt, gl, c, c).transpose(0, 1, 2, 4, 3).reshape(ch, nt, LANES, c)
    rev = ch - 1 - jnp.arange(ch)
    wr, wi = _cmul(pr[rev][..., None], pi[rev][..., None], bbr[None], bbi[None])
    cols = lambda w: w.reshape(ch, nt, gl, p, c).transpose(1, 0, 2, 4, 3).reshape(nt, ch * LANES, p)
    rows = lambda w: w.reshape(ch, nt, gl, c, p).transpose(1, 4, 0, 2, 3).reshape(nt, p, ch * LANES)
    big = jax.ShapeDtypeStruct((nt, ch * LANES, ch * LANES), BF16)
    tile3 = lambda a: pl.BlockSpec((1,) + a.shape[1:], lambda j: (j, 0, 0))
    args = (kt, cols(wr), cols(wi), rows(car[1:]), rows(cai[1:]))
    kbig, bst, cb = pl.pallas_call(
        _s5_ops_kernel,
        grid=(nt,),
        in_specs=[pl.BlockSpec((ch, 1, LANES, c), lambda j: (0, j, 0, 0))] + [tile3(a) for a in args[1:]],
        out_specs=[tile3(big)] * 3,
        out_shape=[big] * 3,
        name="s5_ops",
    )(*args)
    pr8, pi8 = pr[ch].reshape(nt, gl * p), pi[ch].reshape(nt, gl * p)
    a_chunk = jnp.stack([jnp.concatenate([pr8, pr8], -1), jnp.concatenate([-pi8, pi8], -1)], axis=1)
    return kbig, bst, cb, a_chunk


def _s5_ops_kernel(kt_ref, wr_ref, wi_ref, cr_ref, ci_ref, kbig_ref, bst_ref, cb_ref):
    ch = S5_CHUNK
    c = kt_ref.shape[-1]
    p = wr_ref.shape[-1]
    gl = LANES // c
    io = lambda shape, dim: lax.broadcasted_iota(jnp.int32, shape, dim)

    def repeat_lanes(n, reps):
        return (io((n, n * reps), 1) % n == io((n, n * reps), 0)).astype(F32)

    same = io((LANES, LANES), 0) // c == io((LANES, LANES), 1) // c
    taps = [jnp.where(same, _dotf(kt_ref[x, 0], repeat_lanes(c, gl)), 0.0).astype(BF16) for x in range(ch)]
    zero = jnp.zeros((LANES, LANES), BF16)
    for s in range(ch):
        kbig_ref[0, s * LANES:(s + 1) * LANES, :] = jnp.concatenate(
            [taps[t - s] if t >= s else zero for t in range(ch)], axis=1)
    sw = gl * p
    keep = (io((ch * LANES, sw), 0) // c) % gl == io((ch * LANES, sw), 1) // p
    bst_ref[0, :, :sw] = jnp.where(keep, _dotf(wr_ref[0], repeat_lanes(p, gl)), 0.0).astype(BF16)
    bst_ref[0, :, sw:] = jnp.where(keep, _dotf(wi_ref[0], repeat_lanes(p, gl)), 0.0).astype(BF16)
    repeat_rows = (io((sw, p), 0) % p == io((sw, p), 1)).astype(F32)
    keep_t = io((sw, ch * LANES), 0) // p == (io((sw, ch * LANES), 1) // c) % gl
    cb_ref[0, :sw, :] = jnp.where(keep_t, _dotf(repeat_rows, cr_ref[0]), 0.0).astype(BF16)
    cb_ref[0, sw:, :] = jnp.where(keep_t, -_dotf(repeat_rows, ci_ref[0]), 0.0).astype(BF16)


def _s5_kernel(u_ref, kbig_ref, bst_ref, cb_ref, a_ref, y_ref, hl_ref, s_scr, y_scr):
    ch = S5_CHUNK
    nk = u_ref.shape[1] // ch
    ucat = jnp.concatenate([u_ref[0, pl.ds(s, nk, stride=ch), :].astype(BF16) for s in range(ch)], axis=1)
    s_scr[...] = _dot(ucat, bst_ref[0])
    a_same = a_ref[0, 0:1, :]
    a_swap = a_ref[0, 1:2, :]
    half = a_same.shape[-1] // 2

    def rows(gi, h):
        base = pl.multiple_of(gi * SUBLANES, SUBLANES)
        blk = s_scr[pl.ds(base, SUBLANES), :]
        before = []
        for j in range(SUBLANES):
            before.append(h)
            h = h * a_same + pltpu.roll(h, half, 1) * a_swap + blk[j:j + 1, :]
        s_scr[pl.ds(base, SUBLANES), :] = jnp.concatenate(before, axis=0)
        return h

    hl_ref[0, 0] = lax.fori_loop(0, nk // SUBLANES, rows, jnp.zeros((1, 2 * half), F32))
    y = _dot(ucat, kbig_ref[0]) + _dot(s_scr[...].astype(BF16), cb_ref[0])
    for t in range(ch):
        y_scr[pl.ds(t, nk, stride=ch), :] = jax.nn.gelu(y[:, t * LANES:(t + 1) * LANES])
    y_ref[0] = y_scr[...].astype(y_ref.dtype)


def s5_prompt(u, kbig, bst, cb, a_chunk):
    b, l, dm = u.shape
    nt = kbig.shape[0]
    ch = S5_CHUNK
    sw = cb.shape[1]
    tile = lambda a: pl.BlockSpec((1,) + a.shape[1:], lambda j, i: (j, 0, 0))
    y, hl = pl.pallas_call(
        _s5_kernel,
        grid=(nt, b),
        in_specs=[pl.BlockSpec((1, l, LANES), lambda j, i: (i, 0, j)),
                  tile(kbig), tile(bst), tile(cb), tile(a_chunk)],
        out_specs=[pl.BlockSpec((1, l, LANES), lambda j, i: (i, 0, j)),
                   pl.BlockSpec((1, 1, 1, sw), lambda j, i: (i, j, 0, 0))],
        out_shape=[jax.ShapeDtypeStruct((b, l, dm), BF16), jax.ShapeDtypeStruct((b, nt, 1, sw), F32)],
        scratch_shapes=[pltpu.VMEM((l // ch, sw), F32), pltpu.VMEM((l, LANES), F32)],
        name="s5_prompt",
    )(u, kbig, bst, cb, a_chunk)
    hl = hl.reshape(b, nt, 2, -1, SSM_STATE)
    return y, hl[:, :, 0].reshape(b, -1, SSM_STATE), hl[:, :, 1].reshape(b, -1, SSM_STATE)


def _glu_kernel(x_ref, y_ref, w_ref, g_ref, o_ref):
    z = _mm(y_ref[0], w_ref[...])
    d = o_ref.shape[-1]
    o_ref[0] = x_ref[0] + g_ref[0] * (z[:, :d] * jax.nn.sigmoid(z[:, d:]))


def glu_residual(x, y, w_glu, gate, tm):
    b, l, d = x.shape
    row = lambda i, j: (i, j, 0)
    return pl.pallas_call(
        _glu_kernel,
        grid=(b, l // tm),
        in_specs=[pl.BlockSpec((1, tm, d), row), pl.BlockSpec((1, tm, d), row),
                  _weight_spec(w_glu),
                  _mod_spec(gate, tm)],
        out_specs=pl.BlockSpec((1, tm, d), row),
        out_shape=jax.ShapeDtypeStruct((b, l, d), F32),
        name="glu_residual",
    )(x, y, _weight_arr(w_glu), gate)


def _route(logits_t, b_col):
    e_tot = logits_t.shape[0]
    epg = EXPERTS_PER_GROUP
    ng = e_tot // epg
    scores = jax.nn.sigmoid(logits_t)
    sel_all = scores + b_col
    sel = [sel_all[e:e + 1, :] for e in range(e_tot)]
    sc = [scores[e:e + 1, :] for e in range(e_tot)]
    gsum = []
    for g in range(ng):
        a, b, c, d = sel[epg * g:epg * g + epg]
        hi1, lo1, hi2, lo2 = jnp.maximum(a, b), jnp.minimum(a, b), jnp.maximum(c, d), jnp.minimum(c, d)
        gsum.append(jnp.maximum(hi1, hi2) + jnp.maximum(jnp.minimum(hi1, hi2), jnp.maximum(lo1, lo2)))
    best = gsum[0]
    gi = jnp.zeros_like(best, dtype=jnp.int32)
    for g in range(1, ng):
        upd = gsum[g] > best
        gi = jnp.where(upd, g, gi)
        best = jnp.where(upd, gsum[g], best)
    picked = []
    for e in range(e_tot):
        g, j = divmod(e, epg)
        ahead = jnp.zeros_like(gi)
        for j2 in range(epg):
            if j2 == j:
                continue
            o = sel[epg * g + j2]
            before = (o >= sel[e]) if j2 < j else (o > sel[e])
            ahead = ahead + before.astype(jnp.int32)
        picked.append((gi == g) & (ahead < 2))
    wk = [jnp.where(picked[e], sc[e], 0.0) for e in range(e_tot)]
    tot = wk[0]
    for e in range(1, e_tot):
        tot = tot + wk[e]
    return jnp.concatenate([w / tot for w in wk], axis=0)


MOE_SUB = 512


def _router_kernel(x_ref, nw_ref, sh_ref, sc_ref, wr_ref, br_ref, h_ref, g_ref, rank_ref, cnt_ref):
    h = _rms_mod(x_ref[0], nw_ref[...], sh_ref[0], sc_ref[0])
    h_ref[0] = h.astype(h_ref.dtype)
    logits_t = lax.dot_general(wr_ref[...], h, NT_DIMS, preferred_element_type=F32, precision=HIGHEST)
    gates = _route(logits_t, br_ref[...])
    e_tot, l = gates.shape
    sub = min(MOE_SUB, l)
    picked = gates > 0.0
    within = lax.broadcasted_iota(jnp.int32, (e_tot, l), 1) % sub
    c = picked.astype(F32)
    d = 1
    while d < sub:
        c = c + jnp.where(within >= d, pltpu.roll(c, d, 1), 0.0)
        d *= 2
    rank = jnp.where(picked, c - 1.0, -1.0).astype(jnp.int32)
    for e in range(e_tot):
        g_ref[0, e] = gates[e:e + 1, :]
        rank_ref[0, e] = rank[e:e + 1, :]
    cnt_ref[0] = jnp.concatenate([c[:, s * sub + sub - 1:s * sub + sub] for s in range(l // sub)],
                                 axis=1).astype(jnp.int32)


def router(x, nw, shift, scale, w_router, b_router, h_dtype):
    b, l, d = x.shape
    e = w_router.shape[1]
    nsub = l // min(MOE_SUB, l)
    row = lambda i: (i, 0, 0)
    const = lambda i: (0, 0)
    mod = lambda a: pl.BlockSpec((1,) + a.shape[1:], row)
    return pl.pallas_call(
        _router_kernel,
        grid=(b,),
        in_specs=[pl.BlockSpec((1, l, d), row), pl.BlockSpec((1, d), const), mod(shift), mod(scale),
                  pl.BlockSpec((e, d), const), pl.BlockSpec((e, 1), const)],
        out_specs=[pl.BlockSpec((1, l, d), row),
                   pl.BlockSpec((1, e, 1, l), lambda i: (i, 0, 0, 0)),
                   pl.BlockSpec((1, e, 1, l), lambda i: (i, 0, 0, 0)),
                   pl.BlockSpec((1, e, nsub), row)],
        out_shape=[jax.ShapeDtypeStruct((b, l, d), h_dtype), jax.ShapeDtypeStruct((b, e, 1, l), F32),
                   jax.ShapeDtypeStruct((b, e, 1, l), jnp.int32), jax.ShapeDtypeStruct((b, e, nsub), jnp.int32)],
        name="router",
    )(x, nw.reshape(1, d), shift, scale, w_router.T, b_router.reshape(e, 1))


def _moe_dense_kernel(h_ref, gt_ref, wg_ref, wu_ref, wd_ref, x_ref, g6_ref, o_ref, acc_ref):
    e = pl.program_id(2)

    @pl.when(e == 0)
    def _():
        acc_ref[...] = jnp.zeros_like(acc_ref)

    h = h_ref[0]
    hg = _mm(h, wg_ref[0])
    hu = _mm(h, wu_ref[0])
    act = _silu(hg) * hu * gt_ref[0, 0]
    acc_ref[...] += _mm(act, wd_ref[0])

    @pl.when(e == pl.num_programs(2) - 1)
    def _():
        o_ref[0] = x_ref[0] + g6_ref[0] * acc_ref[...]


def moe_dense(h, gates_col, w_gate, w_up, w_down, layer, x, gate6, tm):
    b, l, d = x.shape
    _, e, _, f = w_gate.shape
    row = lambda i, j, k: (i, j, 0)
    wspec = lambda s: pl.BlockSpec((None, 1) + s, lambda i, j, k: (layer, k, 0, 0))
    return pl.pallas_call(
        _moe_dense_kernel,
        grid=(b, l // tm, e),
        in_specs=[pl.BlockSpec((1, tm, d), row),
                  pl.BlockSpec((1, 1, tm, 1), lambda i, j, k: (i, k, j, 0)),
                  wspec((d, f)), wspec((d, f)), wspec((f, d)),
                  pl.BlockSpec((1, tm, d), row),
                  _mod_spec(gate6, tm)],
        out_specs=pl.BlockSpec((1, tm, d), row),
        out_shape=jax.ShapeDtypeStruct((b, l, d), F32),
        scratch_shapes=[pltpu.VMEM((tm, d), F32)],
        name="moe_dense",
    )(h, gates_col, w_gate, w_up, w_down, x, gate6)


MOE_CAP = 128
MOE_TM = 2048
TN_DIMS = (((0,), (0,)), ((), ()))


def _moe_sparse_kernel(cnt_ref, h_ref, rank_ref, gt_ref, wg_ref, wu_ref, wd_ref, x_ref, g6_ref, o_ref):
    bi, j, e = pl.program_id(0), pl.program_id(1), pl.program_id(2)
    tm = o_ref.shape[1]
    sub = min(MOE_SUB, tm)
    nsub = tm // sub
    cap = MOE_CAP

    @pl.when(e == 0)
    def _():
        o_ref[...] = jnp.zeros_like(o_ref)

    row_io = lax.broadcasted_iota(jnp.int32, (cap, sub), 0)

    def chunks(subs, c):
        ps, xs, gs = [], [], []
        for s in subs:
            cols = slice(s * sub, (s + 1) * sub)
            pick = rank_ref[0, 0, :, cols] == row_io + c * cap
            p = pick.astype(BF16)
            ps.append(p)
            xs.append(_dot(p, h_ref[0, cols, :]).astype(BF16))
            gs.append(jnp.sum(jnp.where(pick, gt_ref[0, 0, :, cols], 0.0), axis=1, keepdims=True))
        xg = jnp.concatenate(xs, axis=0)
        act = _silu(_dot(xg, wg_ref[0])) * _dot(xg, wu_ref[0]) * jnp.concatenate(gs, axis=0)
        y = _dot(act.astype(BF16), wd_ref[0]).astype(BF16)
        for n, s in enumerate(subs):
            o_ref[0, s * sub:(s + 1) * sub, :] += lax.dot_general(ps[n], y[n * cap:(n + 1) * cap, :],
                                                                 TN_DIMS, preferred_element_type=F32)

    chunks(list(range(nsub)), 0)
    for s in range(nsub):
        n = cnt_ref[((bi * pl.num_programs(1) + j) * nsub + s) * pl.num_programs(2) + e]

        def more(c, carry, s=s):
            chunks([s], c)
            return carry

        lax.fori_loop(1, (n + cap - 1) // cap, more, 0)

    @pl.when(e == pl.num_programs(2) - 1)
    def _():
        o_ref[0] = x_ref[0] + g6_ref[0] * o_ref[0]


def moe_sparse(h, gates, rank, cnt, w_gate, w_up, w_down, layer, x, gate6):
    b, l, d = x.shape
    _, e, _, f = w_gate.shape
    tm = min(MOE_TM, l)
    row = lambda i, j, k, c: (i, j, 0)
    per_e = lambda i, j, k, c: (i, k, 0, j)
    wsp = lambda s: pl.BlockSpec((None, 1) + s, lambda i, j, k, c: (layer, k, 0, 0))
    grid_spec = pltpu.PrefetchScalarGridSpec(
        num_scalar_prefetch=1,
        grid=(b, l // tm, e),
        in_specs=[pl.BlockSpec((1, tm, d), row),
                  pl.BlockSpec((1, 1, 1, tm), per_e), pl.BlockSpec((1, 1, 1, tm), per_e),
                  wsp((d, f)), wsp((d, f)), wsp((f, d)),
                  pl.BlockSpec((1, tm, d), row),
                  _mod_spec(gate6, tm)],
        out_specs=pl.BlockSpec((1, tm, d), row),
    )
    cnt_flat = cnt.transpose(0, 2, 1).reshape(-1)
    return pl.pallas_call(
        _moe_sparse_kernel,
        grid_spec=grid_spec,
        out_shape=jax.ShapeDtypeStruct((b, l, d), F32),
        name="moe_sparse",
    )(cnt_flat, h, rank, gates, w_gate, w_up, w_down, x, gate6)


def _final_kernel(x_ref, nw_ref, o_ref):
    x = x_ref[0]
    o_ref[0] = x * lax.rsqrt(jnp.mean(x * x, axis=-1, keepdims=True) + EPS) * nw_ref[...]


def final_norm(x, nw, tm):
    b, l, d = x.shape
    row = lambda i, j: (i, j, 0)
    return pl.pallas_call(
        _final_kernel,
        grid=(b, l // tm),
        in_specs=[pl.BlockSpec((1, tm, d), row), pl.BlockSpec((1, d), lambda i, j: (0, 0))],
        out_specs=pl.BlockSpec((1, tm, d), row),
        out_shape=jax.ShapeDtypeStruct((b, l, d), F32),
        name="final_norm",
    )(x, nw.reshape(1, d))


PAGES_PER_STEP = 8


def _sb_attn_sample_kernel(pt_ref, bias_ref, q_ref, *rest):
    npp = PAGES_PER_STEP
    k_refs, v_refs = rest[:npp], rest[npp:2 * npp]
    o_ref, acc_ref, car_ref = rest[2 * npp:]
    s = pl.program_id(1)
    nh = SB_HEADS

    @pl.when(s == 0)
    def _():
        acc_ref[...] = jnp.zeros_like(acc_ref)
        car_ref[...] = jnp.zeros_like(car_ref)

    pg = k_refs[0].shape[-1]
    qb = jnp.broadcast_to(q_ref[0], (nh, SB_HEAD_DIM, pg))
    bias_col = bias_ref[...]
    n = npp * pg
    lane = lax.broadcasted_iota(jnp.int32, (nh, n), 1)
    order = list(range(npp - 1, -1, -1))
    z = jnp.concatenate([jnp.sum(k_refs[r][...] * qb, axis=1) for r in order], axis=1) + bias_col
    sp = _softplus(z)
    l1m = -sp
    c = l1m
    d = 1
    while d < n:
        c = c + jnp.where(lane + d < n, pltpu.roll(c, n - d, 1), 0.0)
        d *= 2
    w = jnp.exp(z - sp + car_ref[:, 0:1] + c - l1m)
    car_ref[...] += jnp.broadcast_to(c[:, 0:1], car_ref.shape)
    acc = acc_ref[...]
    for i, r in enumerate(order):
        acc = acc + v_refs[r][...] * w[:, None, i * pg:(i + 1) * pg]
    acc_ref[...] = acc

    @pl.when(s == pl.num_programs(1) - 1)
    def _():
        o_ref[0] = jnp.sum(acc_ref[...], axis=-1)


def sb_attn_sample(q, cache_kt, cache_vt, layer, page_table, bias):
    n, nh, dh = q.shape
    n_pages = page_table.shape[1]
    pg = cache_kt.shape[-1]
    npp = PAGES_PER_STEP
    nsteps = n_pages // npp

    def page_spec(r):
        def imap(b, s, pt):
            return (layer, pt[b * n_pages + (n_pages - 1 - (s * npp + r))], 0, 0, 0)
        return pl.BlockSpec((None, None, nh, dh, pg), imap)

    grid_spec = pltpu.PrefetchScalarGridSpec(
        num_scalar_prefetch=1,
        grid=(n, nsteps),
        in_specs=([pl.BlockSpec((nh, 1), lambda b, s, pt: (0, 0)),
                   pl.BlockSpec((1, nh, dh, 1), lambda b, s, pt: (b, 0, 0, 0))]
                  + [page_spec(r) for r in range(npp)] * 2),
        out_specs=pl.BlockSpec((1, nh, dh), lambda b, s, pt: (b, 0, 0)),
        scratch_shapes=[pltpu.VMEM((nh, dh, pg), F32), pltpu.VMEM((nh, pg), F32)],
    )
    return pl.pallas_call(
        _sb_attn_sample_kernel,
        grid_spec=grid_spec,
        out_shape=jax.ShapeDtypeStruct((n, nh, dh), F32),
        name="sb_attn_sample",
    )(page_table.reshape(-1), bias.reshape(nh, 1), q.reshape(n, nh, dh, 1),
      *([cache_kt] * npp), *([cache_vt] * npp))


def _conv_sample_kernel(u_ref, st_ref, w_ref, b_ref, lnw_ref, lnb_ref, o_ref):
    km1 = CONV_K - 1
    w = w_ref[...]
    y = jnp.sum(st_ref[...] * w[None, :km1, :], axis=1) + u_ref[...] * w[km1:, :] + b_ref[...]
    o_ref[...] = _layernorm_silu(y, lnw_ref[...], lnb_ref[...])


def conv_sample(u, state, conv_w, conv_b, ln_w, ln_b):
    n, c = u.shape
    return pl.pallas_call(
        _conv_sample_kernel,
        out_shape=jax.ShapeDtypeStruct((n, c), F32),
        name="conv_sample",
    )(u, state, conv_w, conv_b.reshape(1, c), ln_w.reshape(1, c), ln_b.reshape(1, c))


def _s5_sample_kernel(u_ref, h0r_ref, h0i_ref, ar_ref, ai_ref, bbr_ref, bbi_ref, cr_ref, ci_ref, d_ref,
                      y_ref, hr_ref, hi_ref):
    u = u_ref[...]
    ein = functools.partial(jnp.einsum, precision=HIGHEST, preferred_element_type=F32)
    hr0, hi0 = h0r_ref[...], h0i_ref[...]
    ar, ai = ar_ref[...], ai_ref[...]
    hr = ar * hr0 - ai * hi0 + ein('gnc,gcp->gnp', u, bbr_ref[...])
    hi = ar * hi0 + ai * hr0 + ein('gnc,gcp->gnp', u, bbi_ref[...])
    hr_ref[...] = hr
    hi_ref[...] = hi
    y = ein('gnp,gpc->gnc', hr, cr_ref[...]) - ein('gnp,gpc->gnc', hi, ci_ref[...]) + d_ref[...] * u
    y_ref[...] = jax.nn.gelu(y)


def s5_sample(u, h0_re, h0_im, ar, ai, bbr, bbi, c_re, c_im, d):
    n, dm = u.shape
    g, p, c = bbr.shape
    t = lambda a: a.transpose(1, 0, 2)
    sh = jax.ShapeDtypeStruct
    y, hr, hi = pl.pallas_call(
        _s5_sample_kernel,
        out_shape=[sh((g, n, c), F32), sh((g, n, p), F32), sh((g, n, p), F32)],
        name="s5_sample",
    )(t(u.reshape(n, g, c)), t(h0_re), t(h0_im), ar[:, None, :], ai[:, None, :],
      bbr.transpose(0, 2, 1), bbi.transpose(0, 2, 1), c_re.transpose(0, 2, 1), c_im.transpose(0, 2, 1),
      d.reshape(g, 1, c))
    return t(y).reshape(n, dm), t(hr), t(hi)


def kernel(x_prompt, x_sample, c_prompt, c_sample, cache_k, cache_v, page_table, state_conv, state_ssm_re, state_ssm_im, w_ada, b_ada, norm_w, final_norm_w, w_in_mix, w_out_mix, sb_bias, conv_w, conv_b, conv_ln_w, conv_ln_b, ssm_lambda_re, ssm_lambda_im, ssm_b_re, ssm_b_im, ssm_c_re, ssm_c_im, ssm_d, ssm_log_dt, w_glu, w_router, b_router, w_gate, w_up, w_down):
    nb, l, d = x_prompt.shape
    ns = x_sample.shape[0]
    depth = w_ada.shape[0]
    n_attn = cache_k.shape[0]
    assert nb == SUBLANES and x_sample.shape[1] == 1
    tm = 512

    ada = ada_all(jnp.concatenate([c_prompt, c_sample], axis=0), w_ada, b_ada)
    ada = ada.reshape(depth, nb + ns, N_ADA, d)
    cache_kt = cache_k.transpose(0, 1, 3, 4, 2)
    cache_vt = cache_v.transpose(0, 1, 3, 4, 2)

    w_gate_b, w_up_b, w_down_b = w_gate.astype(BF16), w_up.astype(BF16), w_down.astype(BF16)

    xp = x_prompt
    xs = x_sample.reshape(1, ns, d)
    kp_l, vp_l, ks_l, vs_l, cvp_l, cvs_l = [], [], [], [], [], []
    srp_l, sip_l, srs_l, sis_l = [], [], [], []
    for layer in range(depth):
        ap = [ada[layer, :nb, j].reshape(nb, 1, d) for j in range(N_ADA)]
        asm = [ada[layer, nb:, j].reshape(1, ns, d) for j in range(N_ADA)]
        nw = norm_w[layer]
        i = layer // 2
        if layer % 2 == 0:
            w_in_b, w_out_b = w_in_mix[i].astype(BF16), w_out_mix[i].astype(BF16)
            q, k, v, u = inproj(xp, nw[0], ap[0], ap[1], w_in_b, tm, BF16, Q_SCALE * LOG2E)
            oa = sb_attn_prompt(q, k, v, sb_bias[i])
            oc = conv_prompt(u, conv_w[i], conv_b[i], conv_ln_w[i], conv_ln_b[i])
            xp = outproj(xp, oa, oc, w_out_b, ap[2], tm)
            kp_l.append(k.reshape(nb, l, SB_HEADS, SB_HEAD_DIM))
            vp_l.append(v.reshape(nb, l, SB_HEADS, SB_HEAD_DIM))
            cvp_l.append(u[:, l - (CONV_K - 1):, :])

            qs, ks_, vs_, us = inproj(xs, nw[0], asm[0], asm[1], (w_in_mix, i), ns, F32, Q_SCALE)
            oas = sb_attn_sample(qs.reshape(ns, SB_HEADS, SB_HEAD_DIM), cache_kt, cache_vt, i, page_table, sb_bias[i])
            ocs = conv_sample(us[0], state_conv[i], conv_w[i], conv_b[i], conv_ln_w[i], conv_ln_b[i])
            xs = outproj(xs, oas.reshape(1, ns, SB_WIDTH), ocs[None], (w_out_mix, i), asm[2], ns)
            ks_l.append(ks_.reshape(ns, 1, SB_HEADS, SB_HEAD_DIM))
            vs_l.append(vs_.reshape(ns, 1, SB_HEADS, SB_HEAD_DIM))
            cvs_l.append(jnp.concatenate([state_conv[i][:, 1:, :], us.reshape(ns, 1, CONV_WIDTH)], axis=1))
        else:
            ar, ai, bbr, bbi = s5_discretize(ssm_lambda_re[i], ssm_lambda_im[i], ssm_b_re[i], ssm_b_im[i],
                                             ssm_log_dt[i])
            s5_ops = s5_chunk_operators(ar, ai, bbr, bbi, ssm_c_re[i], ssm_c_im[i], ssm_d[i])
            hp = normmod(xp, nw[0], ap[0], ap[1], tm, F32)
            yp, srp, sip = s5_prompt(hp, *s5_ops)
            xp = glu_residual(xp, yp, w_glu[i].astype(BF16), ap[2], tm)
            srp_l.append(srp)
            sip_l.append(sip)

            hs = normmod(xs, nw[0], asm[0], asm[1], ns, F32)
            ys, srs, sis = s5_sample(hs[0], state_ssm_re[i], state_ssm_im[i], ar, ai, bbr, bbi,
                                     ssm_c_re[i], ssm_c_im[i], ssm_d[i])
            xs = glu_residual(xs, ys[None], (w_glu, i), asm[2], ns)
            srs_l.append(srs)
            sis_l.append(sis)
        hp, gates, rank, cnt = router(xp, nw[1], ap[3], ap[4], w_router, b_router, BF16)
        xp = moe_sparse(hp, gates, rank, cnt, w_gate_b, w_up_b, w_down_b, layer, xp, ap[5])
        hs, gates, _, _ = router(xs, nw[1], asm[3], asm[4], w_router, b_router, F32)
        xs = moe_dense(hs, gates.reshape(1, -1, ns, 1), w_gate, w_up, w_down, layer, xs, asm[5], ns)
    y_prompt = final_norm(xp, final_norm_w, tm)
    y_sample = final_norm(xs, final_norm_w, ns).reshape(ns, 1, d)
    return (y_prompt, y_sample,
            jnp.stack(kp_l), jnp.stack(vp_l), jnp.stack(ks_l), jnp.stack(vs_l),
            jnp.stack(cvp_l), jnp.stack(cvs_l),
            jnp.stack(srp_l), jnp.stack(sip_l), jnp.stack(srs_l), jnp.stack(sis_l))
```

```python
import functools
import math

import jax
import jax.numpy as jnp
from jax import lax
from jax.experimental import pallas as pl
from jax.experimental.pallas import tpu as pltpu

F32 = jnp.float32
BF16 = jnp.bfloat16
HIGHEST = lax.Precision.HIGHEST

EPS = 1e-6
LN_EPS = 1e-5
SB_HEADS = 8
SB_HEAD_DIM = 64
SB_WIDTH = SB_HEADS * SB_HEAD_DIM
CONV_WIDTH = 512
CONV_K = 31
SSM_GROUP = 16
SSM_STATE = 64
N_EXPERTS = 16
EXPERTS_PER_GROUP = 4
N_ADA = 6

LANES = 128
SUBLANES = 8
S5_CHUNK = SUBLANES
NT_DIMS = (((1,), (1,)), ((), ()))


def _dot(a, b):
    return jnp.dot(a, b, preferred_element_type=F32)


def _dotf(a, b):
    return jnp.dot(a, b, preferred_element_type=F32, precision=HIGHEST)


def _mm(a, w):
    if w.dtype == BF16:
        return _dot(a.astype(BF16), w)
    return _dotf(a, w)


def _silu(x):
    return x * jax.nn.sigmoid(x)


def _rms_mod(x, nw, shift, scale):
    y = x * lax.rsqrt(jnp.mean(x * x, axis=-1, keepdims=True) + EPS) * nw
    return y * (1 + scale) + shift


def _mod_spec(a, tm):
    d = a.shape[-1]
    if a.shape[1] == 1:
        return pl.BlockSpec((1, 1, d), lambda i, j, *_: (i, 0, 0))
    return pl.BlockSpec((1, tm, d), lambda i, j, *_: (i, j, 0))


def _weight_spec(w):
    if isinstance(w, tuple):
        stack, layer = w
        return pl.BlockSpec((None,) + stack.shape[1:], lambda i, j: (layer, 0, 0))
    return pl.BlockSpec(w.shape, lambda i, j: (0, 0))


def _weight_arr(w):
    return w[0] if isinstance(w, tuple) else w


def _split_bf16(x):
    hi = x.astype(BF16)
    lo = (x - hi.astype(F32)).astype(BF16)
    return hi, lo


def _ada_kernel(c_ref, w_ref, b_ref, o_ref):
    o_ref[0] = _dotf(_silu(c_ref[...]), w_ref[0]) + b_ref[0]


def ada_all(c, w_ada, b_ada):
    depth, d, n = w_ada.shape
    r = c.shape[0]
    tn = 1024
    return pl.pallas_call(
        _ada_kernel,
        grid=(depth, n // tn),
        in_specs=[pl.BlockSpec((r, d), lambda l, j: (0, 0)),
                  pl.BlockSpec((1, d, tn), lambda l, j: (l, 0, j)),
                  pl.BlockSpec((1, 1, tn), lambda l, j: (l, 0, j))],
        out_specs=pl.BlockSpec((1, r, tn), lambda l, j: (l, 0, j)),
        out_shape=jax.ShapeDtypeStruct((depth, r, n), F32),
        name="ada",
    )(c, w_ada, b_ada.reshape(depth, 1, n))


LOG2E = math.log2(math.e)
Q_SCALE = 1.0 / math.sqrt(SB_HEAD_DIM)


def _inproj_kernel(q_scale, x_ref, nw_ref, sh_ref, sc_ref, w_ref, q_ref, k_ref, v_ref, u_ref):
    h = _rms_mod(x_ref[0], nw_ref[...], sh_ref[0], sc_ref[0])
    p = _mm(h, w_ref[...])
    w = SB_WIDTH
    q_ref[0] = (p[:, :w] * q_scale).astype(q_ref.dtype)
    k_ref[0] = p[:, w:2 * w]
    v_ref[0] = p[:, 2 * w:3 * w]
    a = p[:, 3 * w:3 * w + CONV_WIDTH]
    g = p[:, 3 * w + CONV_WIDTH:]
    u_ref[0] = a * jax.nn.sigmoid(g)


def inproj(x, nw, shift, scale, w_in, tm, q_dtype, q_scale):
    b, l, d = x.shape
    row = lambda i, j: (i, j, 0)
    o = jax.ShapeDtypeStruct((b, l, SB_WIDTH), F32)
    return pl.pallas_call(
        functools.partial(_inproj_kernel, q_scale),
        grid=(b, l // tm),
        in_specs=[pl.BlockSpec((1, tm, d), row),
                  pl.BlockSpec((1, d), lambda i, j: (0, 0)),
                  _mod_spec(shift, tm), _mod_spec(scale, tm),
                  _weight_spec(w_in)],
        out_specs=[pl.BlockSpec((1, tm, SB_WIDTH), row)] * 4,
        out_shape=[jax.ShapeDtypeStruct((b, l, SB_WIDTH), q_dtype), o, o, o],
        name="inproj",
    )(x, nw.reshape(1, d), shift, scale, _weight_arr(w_in))


ATT_T = 512
ATT_SUB = LANES


def _softplus(z):
    return jnp.maximum(z, 0.0) + jnp.log(1.0 + jnp.exp(-jnp.abs(z)))


def _softplus2(z2):
    return jnp.maximum(z2, 0.0) + jnp.log2(1.0 + jnp.exp2(-jnp.abs(z2)))


def _sb_attn_kernel(bias_ref, q_ref, k_ref, v_ref, o_ref, acc_ref, run_ref):
    t, sub = ATT_T, ATT_SUB
    hp = pl.program_id(1)
    i = pl.program_id(2)
    q2 = q_ref[0]
    lane = lax.broadcasted_iota(jnp.int32, (t, LANES), 1)
    head_of_lane = lane // SB_HEAD_DIM
    qm = [jnp.where(head_of_lane == hh, q2, jnp.zeros_like(q2)) for hh in range(2)]
    bias = [bias_ref[2 * hp + hh] * LOG2E for hh in range(2)]
    r_io = lax.broadcasted_iota(jnp.int32, (sub, 2 * sub), 0)
    c_io = lax.broadcasted_iota(jnp.int32, (sub, 2 * sub), 1)
    tri_ones = ((r_io > c_io) | (c_io >= sub)).astype(BF16)
    acc_ref[...] = jnp.zeros_like(acc_ref)
    run_ref[...] = jnp.zeros_like(run_ref)
    causal = lax.broadcasted_iota(jnp.int32, (t, t), 1) < lax.broadcasted_iota(jnp.int32, (t, t), 0)

    def chunk(j, masked):
        start = pl.multiple_of(j * t, t)
        kb = k_ref[0, pl.ds(start, t), :].astype(BF16)
        vb = v_ref[0, pl.ds(start, t), :].astype(BF16)
        for hh in range(2):
            z = lax.dot_general(qm[hh], kb, NT_DIMS, preferred_element_type=F32) + bias[hh]
            run = run_ref[hh]
            ws = []
            for blk in range(t // sub - 1, -1, -1):
                cols = slice(blk * sub, (blk + 1) * sub)
                r0 = blk * sub if masked else 0
                zb = z[r0:, cols]
                sp = _softplus2(zb)
                if masked:
                    sp = jnp.where(causal[r0:, cols], sp, 0.0)
                sums = _dot(sp.astype(BF16), tri_ones)
                w = jnp.exp2(zb - sp - (run[r0:] + sums[:, :sub]))
                if masked:
                    w = jnp.where(causal[r0:, cols], w, 0.0)
                w = w.astype(BF16)
                later = run[r0:] + sums[:, sub:]
                if r0:
                    w = jnp.concatenate([jnp.zeros((r0, sub), BF16), w], axis=0)
                    later = jnp.concatenate([run[:r0], later], axis=0)
                ws.append(w)
                run = later
            run_ref[hh] = run
            acc_ref[hh] += _dot(jnp.concatenate(ws[::-1], axis=1), vb)

    chunk(i, True)

    def body(n, c):
        chunk(i - 1 - n, False)
        return c

    lax.fori_loop(0, i, body, 0)
    o_ref[0] = jnp.where(head_of_lane == 0, acc_ref[0], acc_ref[1]).astype(o_ref.dtype)


def sb_attn_prompt(q, k, v, bias):
    b, l, w = q.shape
    t = ATT_T
    return pl.pallas_call(
        _sb_attn_kernel,
        grid=(b, w // LANES, l // t),
        in_specs=[pl.BlockSpec(memory_space=pltpu.SMEM),
                  pl.BlockSpec((1, t, LANES), lambda bi, hp, i: (bi, i, hp)),
                  pl.BlockSpec((1, l, LANES), lambda bi, hp, i: (bi, 0, hp)),
                  pl.BlockSpec((1, l, LANES), lambda bi, hp, i: (bi, 0, hp))],
        out_specs=pl.BlockSpec((1, t, LANES), lambda bi, hp, i: (bi, i, hp)),
        out_shape=jax.ShapeDtypeStruct((b, l, w), BF16),
        scratch_shapes=[pltpu.VMEM((2, t, LANES), F32), pltpu.VMEM((2, t, ATT_SUB), F32)],
        name="sb_attn_prompt",
    )(bias, q, k, v)


CONV_TR = 128
CONV_PAD = 32


def _layernorm_silu(y, lnw, lnb):
    mu = jnp.mean(y, axis=-1, keepdims=True)
    yc = y - mu
    var = jnp.mean(yc * yc, axis=-1, keepdims=True)
    return _silu(yc * lax.rsqrt(var + LN_EPS) * lnw + lnb)


def _conv_kernel(u_ref, w_ref, b_ref, lnw_ref, lnb_ref, o_ref, full_ref):
    l = u_ref.shape[1]
    tr = CONV_TR
    full_ref[0:CONV_PAD, :] = jnp.zeros((CONV_PAD, CONV_WIDTH), F32)
    full_ref[CONV_PAD:CONV_PAD + l, :] = u_ref[0]
    full_ref[CONV_PAD + l:, :] = jnp.zeros((SUBLANES, CONV_WIDTH), F32)
    first = CONV_PAD - (CONV_K - 1)
    bias = b_ref[...]
    lnw = lnw_ref[...]
    lnb = lnb_ref[...]

    def tile(t, c):
        r0 = pl.multiple_of(t * tr, tr)
        y = jnp.zeros((tr, CONV_WIDTH), F32)
        for r in range(SUBLANES):
            g = None
            for m in range(first, first + CONV_K):
                if m % SUBLANES != r:
                    continue
                win = full_ref[pl.ds(r0 + (m - r), tr + SUBLANES), :]
                term = w_ref[m - first:m - first + 1, :] * win
                g = term if g is None else g + term
            if g is not None:
                y = y + g[r:r + tr, :]
        o_ref[0, pl.ds(r0, tr), :] = _layernorm_silu(y + bias, lnw, lnb).astype(o_ref.dtype)
        return c

    lax.fori_loop(0, l // tr, tile, 0)


def conv_prompt(u, conv_w, conv_b, ln_w, ln_b):
    b, l, c = u.shape
    vec = lambda i: (0, 0)
    return pl.pallas_call(
        _conv_kernel,
        grid=(b,),
        in_specs=[pl.BlockSpec((1, l, c), lambda i: (i, 0, 0)),
                  pl.BlockSpec((CONV_K, c), vec),
                  pl.BlockSpec((1, c), vec), pl.BlockSpec((1, c), vec), pl.BlockSpec((1, c), vec)],
        out_specs=pl.BlockSpec((1, l, c), lambda i: (i, 0, 0)),
        out_shape=jax.ShapeDtypeStruct((b, l, c), BF16),
        scratch_shapes=[pltpu.VMEM((CONV_PAD + l + SUBLANES, c), F32)],
        name="conv_prompt",
    )(u, conv_w, conv_b.reshape(1, c), ln_w.reshape(1, c), ln_b.reshape(1, c))


def _outproj_kernel(x_ref, oa_ref, oc_ref, w_ref, g_ref, o_ref):
    m = _mm(oa_ref[0], w_ref[:SB_WIDTH, :]) + _mm(oc_ref[0], w_ref[SB_WIDTH:, :])
    o_ref[0] = x_ref[0] + g_ref[0] * m


def outproj(x, oa, oc, w_out, gate, tm):
    b, l, d = x.shape
    row = lambda i, j: (i, j, 0)
    return pl.pallas_call(
        _outproj_kernel,
        grid=(b, l // tm),
        in_specs=[pl.BlockSpec((1, tm, d), row),
                  pl.BlockSpec((1, tm, SB_WIDTH), row),
                  pl.BlockSpec((1, tm, CONV_WIDTH), row),
                  _weight_spec(w_out),
                  _mod_spec(gate, tm)],
        out_specs=pl.BlockSpec((1, tm, d), row),
        out_shape=jax.ShapeDtypeStruct((b, l, d), F32),
        name="outproj",
    )(x, oa, oc, _weight_arr(w_out), gate)


def _normmod_kernel(x_ref, nw_ref, sh_ref, sc_ref, o_ref):
    o_ref[0] = _rms_mod(x_ref[0], nw_ref[...], sh_ref[0], sc_ref[0]).astype(o_ref.dtype)


def normmod(x, nw, shift, scale, tm, dtype):
    b, l, d = x.shape
    row = lambda i, j: (i, j, 0)
    return pl.pallas_call(
        _normmod_kernel,
        grid=(b, l // tm),
        in_specs=[pl.BlockSpec((1, tm, d), row), pl.BlockSpec((1, d), lambda i, j: (0, 0)),
                  _mod_spec(shift, tm), _mod_spec(scale, tm)],
        out_specs=pl.BlockSpec((1, tm, d), row),
        out_shape=jax.ShapeDtypeStruct((b, l, d), dtype),
        name="normmod",
    )(x, nw.reshape(1, d), shift, scale)


def _cmul(ar, ai, br, bi):
    return ar * br - ai * bi, ar * bi + ai * br


def s5_discretize(lam_re, lam_im, b_re, b_im, log_dt):
    dt = jnp.exp(log_dt)[:, None]
    mag = jnp.exp(lam_re * dt)
    ar, ai = mag * jnp.cos(lam_im * dt), mag * jnp.sin(lam_im * dt)
    den = lam_re * lam_re + lam_im * lam_im
    fr = ((ar - 1) * lam_re + ai * lam_im) / den
    fi = (ai * lam_re - (ar - 1) * lam_im) / den
    bbr, bbi = _cmul(fr[..., None], fi[..., None], b_re, b_im)
    return ar, ai, bbr, bbi


def s5_chunk_operators(ar, ai, bbr, bbi, c_re, c_im, d):
    g, p, c = bbr.shape
    ch = S5_CHUNK
    gl = LANES // c
    nt = g // gl
    pr, pi = [jnp.ones_like(ar)], [jnp.zeros_like(ai)]
    for _ in range(ch):
        nr, ni = _cmul(pr[-1], pi[-1], ar, ai)
        pr.append(nr)
        pi.append(ni)
    pr, pi = jnp.stack(pr), jnp.stack(pi)
    car, cai = _cmul(c_re[None], c_im[None], pr[:, :, None, :], pi[:, :, None, :])
    ein = functools.partial(jnp.einsum, precision=HIGHEST)
    ktau = ein('tgcp,gpd->tgcd', car[:ch], bbr) - ein('tgcp,gpd->tgcd', cai[:ch], bbi)
    ktau = ktau.at[0].add(jnp.eye(c, dtype=F32)[None] * d.reshape(g, c)[:, :, None])
    kt = ktau.reshape(ch, nt, gl, c, c).transpose(0, 1, 2, 4, 3).reshape(ch, nt, LANES, c)
    rev = ch - 1 - jnp.arange(ch)
    wr, wi = _cmul(pr[rev][..., None], pi[rev][..., None], bbr[None], bbi[None])
    cols = lambda w: w.reshape(ch, nt, gl, p, c).transpose(1, 0, 2, 4, 3).reshape(nt, ch * LANES, p)
    rows = lambda w: w.reshape(ch, nt, gl, c, p).transpose(1, 4, 0, 2, 3).reshape(nt, p, ch * LANES)
    big = jax.ShapeDtypeStruct((nt, ch * LANES, ch * LANES), BF16)
    tile3 = lambda a: pl.BlockSpec((1,) + a.shape[1:], lambda j: (j, 0, 0))
    args = (kt, cols(wr), cols(wi), rows(car[1:]), rows(cai[1:]))
    kbig, bst, cb = pl.pallas_call(
        _s5_ops_kernel,
        grid=(nt,),
        in_specs=[pl.BlockSpec((ch, 1, LANES, c), lambda j: (0, j, 0, 0))] + [tile3(a) for a in args[1:]],
        out_specs=[tile3(big)] * 3,
        out_shape=[big] * 3,
        name="s5_ops",
    )(*args)
    pr8, pi8 = pr[ch].reshape(nt, gl * p), pi[ch].reshape(nt, gl * p)
    a_chunk = jnp.stack([jnp.concatenate([pr8, pr8], -1), jnp.concatenate([-pi8, pi8], -1)], axis=1)
    return kbig, bst, cb, a_chunk


def _s5_ops_kernel(kt_ref, wr_ref, wi_ref, cr_ref, ci_ref, kbig_ref, bst_ref, cb_ref):
    ch = S5_CHUNK
    c = kt_ref.shape[-1]
    p = wr_ref.shape[-1]
    gl = LANES // c
    io = lambda shape, dim: lax.broadcasted_iota(jnp.int32, shape, dim)

    def repeat_lanes(n, reps):
        return (io((n, n * reps), 1) % n == io((n, n * reps), 0)).astype(F32)

    same = io((LANES, LANES), 0) // c == io((LANES, LANES), 1) // c
    taps = [jnp.where(same, _dotf(kt_ref[x, 0], repeat_lanes(c, gl)), 0.0).astype(BF16) for x in range(ch)]
    zero = jnp.zeros((LANES, LANES), BF16)
    for s in range(ch):
        kbig_ref[0, s * LANES:(s + 1) * LANES, :] = jnp.concatenate(
            [taps[t - s] if t >= s else zero for t in range(ch)], axis=1)
    sw = gl * p
    keep = (io((ch * LANES, sw), 0) // c) % gl == io((ch * LANES, sw), 1) // p
    bst_ref[0, :, :sw] = jnp.where(keep, _dotf(wr_ref[0], repeat_lanes(p, gl)), 0.0).astype(BF16)
    bst_ref[0, :, sw:] = jnp.where(keep, _dotf(wi_ref[0], repeat_lanes(p, gl)), 0.0).astype(BF16)
    repeat_rows = (io((sw, p), 0) % p == io((sw, p), 1)).astype(F32)
    keep_t = io((sw, ch * LANES), 0) // p == (io((sw, ch * LANES), 1) // c) % gl
    cb_ref[0, :sw, :] = jnp.where(keep_t, _dotf(repeat_rows, cr_ref[0]), 0.0).astype(BF16)
    cb_ref[0, sw:, :] = jnp.where(keep_t, -_dotf(repeat_rows, ci_ref[0]), 0.0).astype(BF16)


def _s5_kernel(u_ref, kbig_ref, bst_ref, cb_ref, a_ref, y_ref, hl_ref, s_scr, y_scr):
    ch = S5_CHUNK
    nk = u_ref.shape[1] // ch
    ucat = jnp.concatenate([u_ref[0, pl.ds(s, nk, stride=ch), :].astype(BF16) for s in range(ch)], axis=1)
    s_scr[...] = _dot(ucat, bst_ref[0])
    a_same = a_ref[0, 0:1, :]
    a_swap = a_ref[0, 1:2, :]
    half = a_same.shape[-1] // 2

    def rows(gi, h):
        base = pl.multiple_of(gi * SUBLANES, SUBLANES)
        blk = s_scr[pl.ds(base, SUBLANES), :]
        before = []
        for j in range(SUBLANES):
            before.append(h)
            h = h * a_same + pltpu.roll(h, half, 1) * a_swap + blk[j:j + 1, :]
        s_scr[pl.ds(base, SUBLANES), :] = jnp.concatenate(before, axis=0)
        return h

    hl_ref[0, 0] = lax.fori_loop(0, nk // SUBLANES, rows, jnp.zeros((1, 2 * half), F32))
    y = _dot(ucat, kbig_ref[0]) + _dot(s_scr[...].astype(BF16), cb_ref[0])
    for t in range(ch):
        y_scr[pl.ds(t, nk, stride=ch), :] = jax.nn.gelu(y[:, t * LANES:(t + 1) * LANES])
    y_ref[0] = y_scr[...].astype(y_ref.dtype)


def s5_prompt(u, kbig, bst, cb, a_chunk):
    b, l, dm = u.shape
    nt = kbig.shape[0]
    ch = S5_CHUNK
    sw = cb.shape[1]
    tile = lambda a: pl.BlockSpec((1,) + a.shape[1:], lambda j, i: (j, 0, 0))
    y, hl = pl.pallas_call(
        _s5_kernel,
        grid=(nt, b),
        in_specs=[pl.BlockSpec((1, l, LANES), lambda j, i: (i, 0, j)),
                  tile(kbig), tile(bst), tile(cb), tile(a_chunk)],
        out_specs=[pl.BlockSpec((1, l, LANES), lambda j, i: (i, 0, j)),
                   pl.BlockSpec((1, 1, 1, sw), lambda j, i: (i, j, 0, 0))],
        out_shape=[jax.ShapeDtypeStruct((b, l, dm), BF16), jax.ShapeDtypeStruct((b, nt, 1, sw), F32)],
        scratch_shapes=[pltpu.VMEM((l // ch, sw), F32), pltpu.VMEM((l, LANES), F32)],
        name="s5_prompt",
    )(u, kbig, bst, cb, a_chunk)
    hl = hl.reshape(b, nt, 2, -1, SSM_STATE)
    return y, hl[:, :, 0].reshape(b, -1, SSM_STATE), hl[:, :, 1].reshape(b, -1, SSM_STATE)


def _glu_kernel(x_ref, y_ref, w_ref, g_ref, o_ref):
    z = _mm(y_ref[0], w_ref[...])
    d = o_ref.shape[-1]
    o_ref[0] = x_ref[0] + g_ref[0] * (z[:, :d] * jax.nn.sigmoid(z[:, d:]))


def glu_residual(x, y, w_glu, gate, tm):
    b, l, d = x.shape
    row = lambda i, j: (i, j, 0)
    return pl.pallas_call(
        _glu_kernel,
        grid=(b, l // tm),
        in_specs=[pl.BlockSpec((1, tm, d), row), pl.BlockSpec((1, tm, d), row),
                  _weight_spec(w_glu),
                  _mod_spec(gate, tm)],
        out_specs=pl.BlockSpec((1, tm, d), row),
        out_shape=jax.ShapeDtypeStruct((b, l, d), F32),
        name="glu_residual",
    )(x, y, _weight_arr(w_glu), gate)


def _route(logits_t, b_col):
    e_tot = logits_t.shape[0]
    epg = EXPERTS_PER_GROUP
    ng = e_tot // epg
    scores = jax.nn.sigmoid(logits_t)
    sel_all = scores + b_col
    sel = [sel_all[e:e + 1, :] for e in range(e_tot)]
    sc = [scores[e:e + 1, :] for e in range(e_tot)]
    gsum = []
    for g in range(ng):
        a, b, c, d = sel[epg * g:epg * g + epg]
        hi1, lo1, hi2, lo2 = jnp.maximum(a, b), jnp.minimum(a, b), jnp.maximum(c, d), jnp.minimum(c, d)
        gsum.append(jnp.maximum(hi1, hi2) + jnp.maximum(jnp.minimum(hi1, hi2), jnp.maximum(lo1, lo2)))
    best = gsum[0]
    gi = jnp.zeros_like(best, dtype=jnp.int32)
    for g in range(1, ng):
        upd = gsum[g] > best
        gi = jnp.where(upd, g, gi)
        best = jnp.where(upd, gsum[g], best)
    picked = []
    for e in range(e_tot):
        g, j = divmod(e, epg)
        ahead = jnp.zeros_like(gi)
        for j2 in range(epg):
            if j2 == j:
                continue
            o = sel[epg * g + j2]
            before = (o >= sel[e]) if j2 < j else (o > sel[e])
            ahead = ahead + before.astype(jnp.int32)
        picked.append((gi == g) & (ahead < 2))
    wk = [jnp.where(picked[e], sc[e], 0.0) for e in range(e_tot)]
    tot = wk[0]
    for e in range(1, e_tot):
        tot = tot + wk[e]
    return jnp.concatenate([w / tot for w in wk], axis=0)


MOE_SUB = 512


def _router_kernel(x_ref, nw_ref, sh_ref, sc_ref, wr_ref, br_ref, h_ref, g_ref, rank_ref, cnt_ref):
    h = _rms_mod(x_ref[0], nw_ref[...], sh_ref[0], sc_ref[0])
    h_ref[0] = h.astype(h_ref.dtype)
    logits_t = lax.dot_general(wr_ref[...], h, NT_DIMS, preferred_element_type=F32, precision=HIGHEST)
    gates = _route(logits_t, br_ref[...])
    e_tot, l = gates.shape
    sub = min(MOE_SUB, l)
    picked = gates > 0.0
    within = lax.broadcasted_iota(jnp.int32, (e_tot, l), 1) % sub
    c = picked.astype(F32)
    d = 1
    while d < sub:
        c = c + jnp.where(within >= d, pltpu.roll(c, d, 1), 0.0)
        d *= 2
    rank = jnp.where(picked, c - 1.0, -1.0).astype(jnp.int32)
    for e in range(e_tot):
        g_ref[0, e] = gates[e:e + 1, :]
        rank_ref[0, e] = rank[e:e + 1, :]
    cnt_ref[0] = jnp.concatenate([c[:, s * sub + sub - 1:s * sub + sub] for s in range(l // sub)],
                                 axis=1).astype(jnp.int32)


def router(x, nw, shift, scale, w_router, b_router, h_dtype):
    b, l, d = x.shape
    e = w_router.shape[1]
    nsub = l // min(MOE_SUB, l)
    row = lambda i: (i, 0, 0)
    const = lambda i: (0, 0)
    mod = lambda a: pl.BlockSpec((1,) + a.shape[1:], row)
    return pl.pallas_call(
        _router_kernel,
        grid=(b,),
        in_specs=[pl.BlockSpec((1, l, d), row), pl.BlockSpec((1, d), const), mod(shift), mod(scale),
                  pl.BlockSpec((e, d), const), pl.BlockSpec((e, 1), const)],
        out_specs=[pl.BlockSpec((1, l, d), row),
                   pl.BlockSpec((1, e, 1, l), lambda i: (i, 0, 0, 0)),
                   pl.BlockSpec((1, e, 1, l), lambda i: (i, 0, 0, 0)),
                   pl.BlockSpec((1, e, nsub), row)],
        out_shape=[jax.ShapeDtypeStruct((b, l, d), h_dtype), jax.ShapeDtypeStruct((b, e, 1, l), F32),
                   jax.ShapeDtypeStruct((b, e, 1, l), jnp.int32), jax.ShapeDtypeStruct((b, e, nsub), jnp.int32)],
        name="router",
    )(x, nw.reshape(1, d), shift, scale, w_router.T, b_router.reshape(e, 1))


def _moe_dense_kernel(h_ref, gt_ref, wg_ref, wu_ref, wd_ref, x_ref, g6_ref, o_ref, acc_ref):
    e = pl.program_id(2)

    @pl.when(e == 0)
    def _():
        acc_ref[...] = jnp.zeros_like(acc_ref)

    h = h_ref[0]
    hg = _mm(h, wg_ref[0])
    hu = _mm(h, wu_ref[0])
    act = _silu(hg) * hu * gt_ref[0, 0]
    acc_ref[...] += _mm(act, wd_ref[0])

    @pl.when(e == pl.num_programs(2) - 1)
    def _():
        o_ref[0] = x_ref[0] + g6_ref[0] * acc_ref[...]


def moe_dense(h, gates_col, w_gate, w_up, w_down, layer, x, gate6, tm):
    b, l, d = x.shape
    _, e, _, f = w_gate.shape
    row = lambda i, j, k: (i, j, 0)
    wspec = lambda s: pl.BlockSpec((None, 1) + s, lambda i, j, k: (layer, k, 0, 0))
    return pl.pallas_call(
        _moe_dense_kernel,
        grid=(b, l // tm, e),
        in_specs=[pl.BlockSpec((1, tm, d), row),
                  pl.BlockSpec((1, 1, tm, 1), lambda i, j, k: (i, k, j, 0)),
                  wspec((d, f)), wspec((d, f)), wspec((f, d)),
                  pl.BlockSpec((1, tm, d), row),
                  _mod_spec(gate6, tm)],
        out_specs=pl.BlockSpec((1, tm, d), row),
        out_shape=jax.ShapeDtypeStruct((b, l, d), F32),
        scratch_shapes=[pltpu.VMEM((tm, d), F32)],
        name="moe_dense",
    )(h, gates_col, w_gate, w_up, w_down, x, gate6)


MOE_CAP = 128
MOE_TM = 2048
TN_DIMS = (((0,), (0,)), ((), ()))


def _moe_sparse_kernel(cnt_ref, h_ref, rank_ref, gt_ref, wg_ref, wu_ref, wd_ref, x_ref, g6_ref, o_ref):
    bi, j, e = pl.program_id(0), pl.program_id(1), pl.program_id(2)
    tm = o_ref.shape[1]
    sub = min(MOE_SUB, tm)
    nsub = tm // sub
    cap = MOE_CAP

    @pl.when(e == 0)
    def _():
        o_ref[...] = jnp.zeros_like(o_ref)

    row_io = lax.broadcasted_iota(jnp.int32, (cap, sub), 0)

    def chunks(subs, c):
        ps, xs, gs = [], [], []
        for s in subs:
            cols = slice(s * sub, (s + 1) * sub)
            pick = rank_ref[0, 0, :, cols] == row_io + c * cap
            p = pick.astype(BF16)
            ps.append(p)
            xs.append(_dot(p, h_ref[0, cols, :]).astype(BF16))
            gs.append(jnp.sum(jnp.where(pick, gt_ref[0, 0, :, cols], 0.0), axis=1, keepdims=True))
        xg = jnp.concatenate(xs, axis=0)
        act = _silu(_dot(xg, wg_ref[0])) * _dot(xg, wu_ref[0]) * jnp.concatenate(gs, axis=0)
        y = _dot(act.astype(BF16), wd_ref[0]).astype(BF16)
        for n, s in enumerate(subs):
            o_ref[0, s * sub:(s + 1) * sub, :] += lax.dot_general(ps[n], y[n * cap:(n + 1) * cap, :],
                                                                 TN_DIMS, preferred_element_type=F32)

    chunks(list(range(nsub)), 0)
    for s in range(nsub):
        n = cnt_ref[((bi * pl.num_programs(1) + j) * nsub + s) * pl.num_programs(2) + e]

        def more(c, carry, s=s):
            chunks([s], c)
            return carry

        lax.fori_loop(1, (n + cap - 1) // cap, more, 0)

    @pl.when(e == pl.num_programs(2) - 1)
    def _():
        o_ref[0] = x_ref[0] + g6_ref[0] * o_ref[0]


def moe_sparse(h, gates, rank, cnt, w_gate, w_up, w_down, layer, x, gate6):
    b, l, d = x.shape
    _, e, _, f = w_gate.shape
    tm = min(MOE_TM, l)
    row = lambda i, j, k, c: (i, j, 0)
    per_e = lambda i, j, k, c: (i, k, 0, j)
    wsp = lambda s: pl.BlockSpec((None, 1) + s, lambda i, j, k, c: (layer, k, 0, 0))
    grid_spec = pltpu.PrefetchScalarGridSpec(
        num_scalar_prefetch=1,
        grid=(b, l // tm, e),
        in_specs=[pl.BlockSpec((1, tm, d), row),
                  pl.BlockSpec((1, 1, 1, tm), per_e), pl.BlockSpec((1, 1, 1, tm), per_e),
                  wsp((d, f)), wsp((d, f)), wsp((f, d)),
                  pl.BlockSpec((1, tm, d), row),
                  _mod_spec(gate6, tm)],
        out_specs=pl.BlockSpec((1, tm, d), row),
    )
    cnt_flat = cnt.transpose(0, 2, 1).reshape(-1)
    return pl.pallas_call(
        _moe_sparse_kernel,
        grid_spec=grid_spec,
        out_shape=jax.ShapeDtypeStruct((b, l, d), F32),
        name="moe_sparse",
    )(cnt_flat, h, rank, gates, w_gate, w_up, w_down, x, gate6)


def _final_kernel(x_ref, nw_ref, o_ref):
    x = x_ref[0]
    o_ref[0] = x * lax.rsqrt(jnp.mean(x * x, axis=-1, keepdims=True) + EPS) * nw_ref[...]


def final_norm(x, nw, tm):
    b, l, d = x.shape
    row = lambda i, j: (i, j, 0)
    return pl.pallas_call(
        _final_kernel,
        grid=(b, l // tm),
        in_specs=[pl.BlockSpec((1, tm, d), row), pl.BlockSpec((1, d), lambda i, j: (0, 0))],
        out_specs=pl.BlockSpec((1, tm, d), row),
        out_shape=jax.ShapeDtypeStruct((b, l, d), F32),
        name="final_norm",
    )(x, nw.reshape(1, d))


PAGES_PER_STEP = 8


def _sb_attn_sample_kernel(pt_ref, bias_ref, q_ref, *rest):
    npp = PAGES_PER_STEP
    k_refs, v_refs = rest[:npp], rest[npp:2 * npp]
    o_ref, acc_ref, car_ref = rest[2 * npp:]
    s = pl.program_id(1)
    nh = SB_HEADS

    @pl.when(s == 0)
    def _():
        acc_ref[...] = jnp.zeros_like(acc_ref)
        car_ref[...] = jnp.zeros_like(car_ref)

    pg = k_refs[0].shape[-1]
    qb = jnp.broadcast_to(q_ref[0], (nh, SB_HEAD_DIM, pg))
    bias_col = bias_ref[...]
    n = npp * pg
    lane = lax.broadcasted_iota(jnp.int32, (nh, n), 1)
    order = list(range(npp - 1, -1, -1))
    z = jnp.concatenate([jnp.sum(k_refs[r][...] * qb, axis=1) for r in order], axis=1) + bias_col
    sp = _softplus(z)
    l1m = -sp
    c = l1m
    d = 1
    while d < n:
        c = c + jnp.where(lane + d < n, pltpu.roll(c, n - d, 1), 0.0)
        d *= 2
    w = jnp.exp(z - sp + car_ref[:, 0:1] + c - l1m)
    car_ref[...] += jnp.broadcast_to(c[:, 0:1], car_ref.shape)
    acc = acc_ref[...]
    for i, r in enumerate(order):
        acc = acc + v_refs[r][...] * w[:, None, i * pg:(i + 1) * pg]
    acc_ref[...] = acc

    @pl.when(s == pl.num_programs(1) - 1)
    def _():
        o_ref[0] = jnp.sum(acc_ref[...], axis=-1)


def sb_attn_sample(q, cache_kt, cache_vt, layer, page_table, bias):
    n, nh, dh = q.shape
    n_pages = page_table.shape[1]
    pg = cache_kt.shape[-1]
    npp = PAGES_PER_STEP
    nsteps = n_pages // npp

    def page_spec(r):
        def imap(b, s, pt):
            return (layer, pt[b * n_pages + (n_pages - 1 - (s * npp + r))], 0, 0, 0)
        return pl.BlockSpec((None, None, nh, dh, pg), imap)

    grid_spec = pltpu.PrefetchScalarGridSpec(
        num_scalar_prefetch=1,
        grid=(n, nsteps),
        in_specs=([pl.BlockSpec((nh, 1), lambda b, s, pt: (0, 0)),
                   pl.BlockSpec((1, nh, dh, 1), lambda b, s, pt: (b, 0, 0, 0))]
                  + [page_spec(r) for r in range(npp)] * 2),
        out_specs=pl.BlockSpec((1, nh, dh), lambda b, s, pt: (b, 0, 0)),
        scratch_shapes=[pltpu.VMEM((nh, dh, pg), F32), pltpu.VMEM((nh, pg), F32)],
    )
    return pl.pallas_call(
        _sb_attn_sample_kernel,
        grid_spec=grid_spec,
        out_shape=jax.ShapeDtypeStruct((n, nh, dh), F32),
        name="sb_attn_sample",
    )(page_table.reshape(-1), bias.reshape(nh, 1), q.reshape(n, nh, dh, 1),
      *([cache_kt] * npp), *([cache_vt] * npp))


def _conv_sample_kernel(u_ref, st_ref, w_ref, b_ref, lnw_ref, lnb_ref, o_ref):
    km1 = CONV_K - 1
    w = w_ref[...]
    y = jnp.sum(st_ref[...] * w[None, :km1, :], axis=1) + u_ref[...] * w[km1:, :] + b_ref[...]
    o_ref[...] = _layernorm_silu(y, lnw_ref[...], lnb_ref[...])


def conv_sample(u, state, conv_w, conv_b, ln_w, ln_b):
    n, c = u.shape
    return pl.pallas_call(
        _conv_sample_kernel,
        out_shape=jax.ShapeDtypeStruct((n, c), F32),
        name="conv_sample",
    )(u, state, conv_w, conv_b.reshape(1, c), ln_w.reshape(1, c), ln_b.reshape(1, c))


def _s5_sample_kernel(u_ref, h0r_ref, h0i_ref, ar_ref, ai_ref, bbr_ref, bbi_ref, cr_ref, ci_ref, d_ref,
                      y_ref, hr_ref, hi_ref):
    u = u_ref[...]
    ein = functools.partial(jnp.einsum, precision=HIGHEST, preferred_element_type=F32)
    hr0, hi0 = h0r_ref[...], h0i_ref[...]
    ar, ai = ar_ref[...], ai_ref[...]
    hr = ar * hr0 - ai * hi0 + ein('gnc,gcp->gnp', u, bbr_ref[...])
    hi = ar * hi0 + ai * hr0 + ein('gnc,gcp->gnp', u, bbi_ref[...])
    hr_ref[...] = hr
    hi_ref[...] = hi
    y = ein('gnp,gpc->gnc', hr, cr_ref[...]) - ein('gnp,gpc->gnc', hi, ci_ref[...]) + d_ref[...] * u
    y_ref[...] = jax.nn.gelu(y)


def s5_sample(u, h0_re, h0_im, ar, ai, bbr, bbi, c_re, c_im, d):
    n, dm = u.shape
    g, p, c = bbr.shape
    t = lambda a: a.transpose(1, 0, 2)
    sh = jax.ShapeDtypeStruct
    y, hr, hi = pl.pallas_call(
        _s5_sample_kernel,
        out_shape=[sh((g, n, c), F32), sh((g, n, p), F32), sh((g, n, p), F32)],
        name="s5_sample",
    )(t(u.reshape(n, g, c)), t(h0_re), t(h0_im), ar[:, None, :], ai[:, None, :],
      bbr.transpose(0, 2, 1), bbi.transpose(0, 2, 1), c_re.transpose(0, 2, 1), c_im.transpose(0, 2, 1),
      d.reshape(g, 1, c))
    return t(y).reshape(n, dm), t(hr), t(hi)


def kernel(x_prompt, x_sample, c_prompt, c_sample, cache_k, cache_v, page_table, state_conv, state_ssm_re, state_ssm_im, w_ada, b_ada, norm_w, final_norm_w, w_in_mix, w_out_mix, sb_bias, conv_w, conv_b, conv_ln_w, conv_ln_b, ssm_lambda_re, ssm_lambda_im, ssm_b_re, ssm_b_im, ssm_c_re, ssm_c_im, ssm_d, ssm_log_dt, w_glu, w_router, b_router, w_gate, w_up, w_down):
    nb, l, d = x_prompt.shape
    ns = x_sample.shape[0]
    depth = w_ada.shape[0]
    n_attn = cache_k.shape[0]
    assert nb == SUBLANES and x_sample.shape[1] == 1
    tm = 512

    ada = ada_all(jnp.concatenate([c_prompt, c_sample], axis=0), w_ada, b_ada)
    ada = ada.reshape(depth, nb + ns, N_ADA, d)
    cache_kt = cache_k.transpose(0, 1, 3, 4, 2)
    cache_vt = cache_v.transpose(0, 1, 3, 4, 2)

    w_gate_b, w_up_b, w_down_b = w_gate.astype(BF16), w_up.astype(BF16), w_down.astype(BF16)

    xp = x_prompt
    xs = x_sample.reshape(1, ns, d)
    kp_l, vp_l, ks_l, vs_l, cvp_l, cvs_l = [], [], [], [], [], []
    srp_l, sip_l, srs_l, sis_l = [], [], [], []
    for layer in range(depth):
        ap = [ada[layer, :nb, j].reshape(nb, 1, d) for j in range(N_ADA)]
        asm = [ada[layer, nb:, j].reshape(1, ns, d) for j in range(N_ADA)]
        nw = norm_w[layer]
        i = layer // 2
        if layer % 2 == 0:
            w_in_b, w_out_b = w_in_mix[i].astype(BF16), w_out_mix[i].astype(BF16)
            q, k, v, u = inproj(xp, nw[0], ap[0], ap[1], w_in_b, tm, BF16, Q_SCALE * LOG2E)
            oa = sb_attn_prompt(q, k, v, sb_bias[i])
            oc = conv_prompt(u, conv_w[i], conv_b[i], conv_ln_w[i], conv_ln_b[i])
            xp = outproj(xp, oa, oc, w_out_b, ap[2], tm)
            kp_l.append(k.reshape(nb, l, SB_HEADS, SB_HEAD_DIM))
            vp_l.append(v.reshape(nb, l, SB_HEADS, SB_HEAD_DIM))
            cvp_l.append(u[:, l - (CONV_K - 1):, :])

            qs, ks_, vs_, us = inproj(xs, nw[0], asm[0], asm[1], (w_in_mix, i), ns, F32, Q_SCALE)
            oas = sb_attn_sample(qs.reshape(ns, SB_HEADS, SB_HEAD_DIM), cache_kt, cache_vt, i, page_table, sb_bias[i])
            ocs = conv_sample(us[0], state_conv[i], conv_w[i], conv_b[i], conv_ln_w[i], conv_ln_b[i])
            xs = outproj(xs, oas.reshape(1, ns, SB_WIDTH), ocs[None], (w_out_mix, i), asm[2], ns)
            ks_l.append(ks_.reshape(ns, 1, SB_HEADS, SB_HEAD_DIM))
            vs_l.append(vs_.reshape(ns, 1, SB_HEADS, SB_HEAD_DIM))
            cvs_l.append(jnp.concatenate([state_conv[i][:, 1:, :], us.reshape(ns, 1, CONV_WIDTH)], axis=1))
        else:
            ar, ai, bbr, bbi = s5_discretize(ssm_lambda_re[i], ssm_lambda_im[i], ssm_b_re[i], ssm_b_im[i],
                                             ssm_log_dt[i])
            s5_ops = s5_chunk_operators(ar, ai, bbr, bbi, ssm_c_re[i], ssm_c_im[i], ssm_d[i])
            hp = normmod(xp, nw[0], ap[0], ap[1], tm, F32)
            yp, srp, sip = s5_prompt(hp, *s5_ops)
            xp = glu_residual(xp, yp, w_glu[i].astype(BF16), ap[2], tm)
            srp_l.append(srp)
            sip_l.append(sip)

            hs = normmod(xs, nw[0], asm[0], asm[1], ns, F32)
            ys, srs, sis = s5_sample(hs[0], state_ssm_re[i], state_ssm_im[i], ar, ai, bbr, bbi,
                                     ssm_c_re[i], ssm_c_im[i], ssm_d[i])
            xs = glu_residual(xs, ys[None], (w_glu, i), asm[2], ns)
            srs_l.append(srs)
            sis_l.append(sis)
        hp, gates, rank, cnt = router(xp, nw[1], ap[3], ap[4], w_router, b_router, BF16)
        xp = moe_sparse(hp, gates, rank, cnt, w_gate_b, w_up_b, w_down_b, layer, xp, ap[5])
        hs, gates, _, _ = router(xs, nw[1], asm[3], asm[4], w_router, b_router, F32)
        xs = moe_dense(hs, gates.reshape(1, -1, ns, 1), w_gate, w_up, w_down, layer, xs, asm[5], ns)
    y_prompt = final_norm(xp, final_norm_w, tm)
    y_sample = final_norm(xs, final_norm_w, ns).reshape(ns, 1, d)
    return (y_prompt, y_sample,
            jnp.stack(kp_l), jnp.stack(vp_l), jnp.stack(ks_l), jnp.stack(vs_l),
            jnp.stack(cvp_l), jnp.stack(cvs_l),
            jnp.stack(srp_l), jnp.stack(sip_l), jnp.stack(srs_l), jnp.stack(sis_l))
```
